```python
import math
import jax, jax.numpy as jnp
from jax import lax
import numpy as np

D_MODEL = 2048
BATCH = 2
SEQ = 4096
DEPTH = 4

PLE_DIM = 256
ATTN_WIDTH = D_MODEL // 2
POOL_WIDTH = D_MODEL - ATTN_WIDTH
HEAD_DIM = 64
N_HEADS = ATTN_WIDTH // HEAD_DIM
N_KV_HEADS = max(1, N_HEADS // 8)
KV_GROUP = N_HEADS // N_KV_HEADS
WINDOW = 128
BLOCK = WINDOW
POOL_WINDOWS = (2, 4, 8, 16)
N_POOL_GROUPS = len(POOL_WINDOWS)
POOL_GROUP_DIM = POOL_WIDTH // N_POOL_GROUPS
REL_BUCKETS = 32
REL_MAX_DIST = 128
LN_EPS = 1e-5
DEEPNORM_ALPHA = (2.0 * DEPTH) ** 0.25
DEEPNORM_BETA = (8.0 * DEPTH) ** -0.25
Q_COLS = N_HEADS * HEAD_DIM
KV_COLS = N_KV_HEADS * HEAD_DIM
SPLIT_SIZES = (Q_COLS, KV_COLS, KV_COLS, ATTN_WIDTH, POOL_WIDTH, POOL_WIDTH)
SPLIT_POINTS = tuple(int(c) for c in np.cumsum(SPLIT_SIZES)[:-1])
IN_COLS = int(sum(SPLIT_SIZES))

kernel_name = "hymba_swa_sink_pool_deepnorm"


def t5_causal_bucket(dist):
    max_exact = REL_BUCKETS // 2
    d = jnp.maximum(dist, 0)
    d_f = jnp.maximum(d, 1).astype(jnp.float32)
    large = max_exact + (jnp.log(d_f / max_exact) / math.log(REL_MAX_DIST / max_exact)
                         * (REL_BUCKETS - max_exact)).astype(jnp.int32)
    large = jnp.minimum(large, REL_BUCKETS - 1)
    return jnp.where(d < max_exact, d, large)


def band_geometry():
    qq = jnp.arange(BLOCK)[:, None]
    kk = jnp.arange(2 * BLOCK)[None, :]
    dist = qq + BLOCK - kk
    in_window = (dist >= 0) & (dist < WINDOW)
    return dist, in_window


def layer_norm(x, gain, bias):
    xf = x.astype(jnp.float32)
    mu = jnp.mean(xf, axis=-1, keepdims=True)
    var = jnp.mean(jnp.square(xf - mu), axis=-1, keepdims=True)
    y = (xf - mu) * lax.rsqrt(var + LN_EPS)
    return (y * gain.astype(jnp.float32) + bias.astype(jnp.float32)).astype(x.dtype)


def banded_sink_attention(q, k, v, sinks, bias_hqk, valid):
    B, S, _ = q.shape
    nblk = S // BLOCK
    qb = q.reshape(B, nblk, BLOCK, N_KV_HEADS, KV_GROUP, HEAD_DIM)
    pad = ((0, 0), (BLOCK, 0), (0, 0))
    kp = jnp.pad(k, pad).reshape(B, nblk + 1, BLOCK, N_KV_HEADS, HEAD_DIM)
    vp = jnp.pad(v, pad).reshape(B, nblk + 1, BLOCK, N_KV_HEADS, HEAD_DIM)
    kb = jnp.concatenate([kp[:, :-1], kp[:, 1:]], axis=2)
    vb = jnp.concatenate([vp[:, :-1], vp[:, 1:]], axis=2)
    scale = 1.0 / math.sqrt(HEAD_DIM)
    scores = jnp.einsum('bnqhgd,bnkhd->bnhgqk', qb, kb).astype(jnp.float32) * scale
    scores = scores + bias_hqk[None, None]
    scores = jnp.where(valid[None, :, None, None], scores, -1e30)
    s_sink = sinks.astype(jnp.float32).reshape(N_KV_HEADS, KV_GROUP)[None, None, :, :, None, None]
    m = jnp.maximum(jnp.max(scores, axis=-1, keepdims=True), s_sink)
    e = jnp.exp(scores - m)
    denom = jnp.sum(e, axis=-1, keepdims=True) + jnp.exp(s_sink - m)
    probs = (e / denom).astype(v.dtype)
    out = jnp.einsum('bnhgqk,bnkhd->bnqhgd', probs, vb)
    return out.reshape(B, S, N_HEADS * HEAD_DIM)


def multiscale_pool(u, w_pool, pool_scale):
    B, S, _ = u.shape
    ug = u.reshape(B, S, N_POOL_GROUPS, POOL_GROUP_DIM).astype(jnp.float32)
    cs = jnp.cumsum(ug, axis=1)
    t = jnp.arange(S)
    means = []
    for g, w in enumerate(POOL_WINDOWS):
        c = cs[:, :, g]
        lagged = jnp.pad(c[:, :S - w], ((0, 0), (w, 0), (0, 0)))
        count = jnp.minimum(t + 1, w).astype(jnp.float32)[None, :, None]
        means.append((c - lagged) / count)
    pooled = jnp.stack(means, axis=2)
    diff = (pooled - ug).astype(u.dtype)
    mixed = jnp.einsum('bsgc,gcd->bsgd', diff, w_pool)
    return mixed.reshape(B, S, POOL_WIDTH) * pool_scale


def hybrid_layer(x, p_i, w_in, b_in, w_out, sinks, w_pool, pool_scale, w_ple, w_gate_ple,
                 ln_gain, ln_bias, bias_hqk, valid):
    h = jnp.einsum('bsd,dc->bsc', x, w_in) + b_in
    q, k, v, g_attn, u_pool, g_pool = jnp.split(h, SPLIT_POINTS, axis=-1)
    a = banded_sink_attention(q, k, v, sinks, bias_hqk, valid) * jax.nn.silu(g_attn)
    b = multiscale_pool(u_pool, w_pool, pool_scale) * jax.nn.silu(g_pool)
    mix = jnp.einsum('bsc,cd->bsd', jnp.concatenate([a, b], axis=-1), w_out)
    ple = jax.nn.sigmoid(jnp.einsum('bsd,de->bse', x, w_gate_ple)) * jnp.einsum('bsp,pd->bsd', p_i, w_ple)
    return layer_norm(DEEPNORM_ALPHA * x + mix + ple, ln_gain, ln_bias)


def setup_inputs(seed: int = 0) -> dict:
    key = jax.random.key(seed)
    ks = jax.random.split(key, 14)
    f32 = jnp.float32
    x = jax.random.normal(ks[0], (BATCH, SEQ, D_MODEL), f32)
    p = jax.random.normal(ks[1], (DEPTH, BATCH, SEQ, PLE_DIM), f32)
    w_in = jax.random.normal(ks[2], (DEPTH, D_MODEL, IN_COLS), f32) * D_MODEL ** -0.5
    b_in = jax.random.normal(ks[3], (DEPTH, IN_COLS), f32) * 0.02
    w_out = jax.random.normal(ks[4], (DEPTH, D_MODEL, D_MODEL), f32) * (D_MODEL ** -0.5 * DEEPNORM_BETA)
    attn_sinks = jax.random.normal(ks[5], (DEPTH, N_HEADS), f32) * 0.5
    rel_bias = jax.random.normal(ks[6], (REL_BUCKETS, N_HEADS), f32) * 0.1
    w_pool = jax.random.normal(ks[7], (DEPTH, N_POOL_GROUPS, POOL_GROUP_DIM, POOL_GROUP_DIM), f32) * POOL_GROUP_DIM ** -0.5
    pool_scale = 1.0 + 0.1 * jax.random.normal(ks[8], (DEPTH, POOL_WIDTH), f32)
    w_ple = jax.random.normal(ks[9], (DEPTH, PLE_DIM, D_MODEL), f32) * PLE_DIM ** -0.5
    w_gate_ple = jax.random.normal(ks[10], (DEPTH, D_MODEL, D_MODEL), f32) * D_MODEL ** -0.5
    ln_gain = 1.0 + 0.02 * jax.random.normal(ks[11], (DEPTH, D_MODEL), f32)
    ln_bias = 0.02 * jax.random.normal(ks[12], (DEPTH, D_MODEL), f32)
    return {"x": x, "p": p, "w_in": w_in, "b_in": b_in, "w_out": w_out,
            "attn_sinks": attn_sinks, "rel_bias": rel_bias, "w_pool": w_pool,
            "pool_scale": pool_scale, "w_ple": w_ple, "w_gate_ple": w_gate_ple,
            "ln_gain": ln_gain, "ln_bias": ln_bias}


def reference(x, p, w_in, b_in, w_out, attn_sinks, rel_bias, w_pool, pool_scale, w_ple,
              w_gate_ple, ln_gain, ln_bias):
    S = x.shape[1]
    nblk = S // BLOCK
    dist, in_window = band_geometry()
    bias_hqk = jnp.transpose(rel_bias[t5_causal_bucket(dist)], (2, 0, 1)).astype(jnp.float32)
    bias_hqk = bias_hqk.reshape(N_KV_HEADS, KV_GROUP, BLOCK, 2 * BLOCK)
    k_pos = (jnp.arange(nblk) * BLOCK - BLOCK)[:, None, None] + jnp.arange(2 * BLOCK)[None, None, :]
    valid = in_window[None] & (k_pos >= 0)
    for i in range(DEPTH):
        x = hybrid_layer(x, p[i], w_in[i], b_in[i], w_out[i], attn_sinks[i], w_pool[i],
                         pool_scale[i], w_ple[i], w_gate_ple[i], ln_gain[i], ln_bias[i],
                         bias_hqk, valid)
    return x
```

```python
import functools
import math

import jax
import jax.numpy as jnp
from jax import lax
from jax.experimental import pallas as pl
from jax.experimental.pallas import tpu as pltpu

D_MODEL = 2048
DEPTH = 4
PLE_DIM = 256
ATTN_WIDTH = D_MODEL // 2
POOL_WIDTH = D_MODEL - ATTN_WIDTH
HEAD_DIM = 64
N_HEADS = ATTN_WIDTH // HEAD_DIM
N_KV_HEADS = max(1, N_HEADS // 8)
KV_GROUP = N_HEADS // N_KV_HEADS
WINDOW = 128
BLOCK = WINDOW
POOL_WINDOWS = (2, 4, 8, 16)
POOL_GROUP_DIM = POOL_WIDTH // len(POOL_WINDOWS)
POOL_TAIL = 16
REL_BUCKETS = 32
REL_MAX_DIST = 128
LN_EPS = 1e-5
DEEPNORM_ALPHA = (2.0 * DEPTH) ** 0.25
Q_COLS = N_HEADS * HEAD_DIM
KV_COLS = N_KV_HEADS * HEAD_DIM
IN_COLS = Q_COLS + 2 * KV_COLS + ATTN_WIDTH + 2 * POOL_WIDTH
ATTN_SCALE = 1.0 / math.sqrt(HEAD_DIM)
MASK_VALUE = -1e30

V7X_LANES = 128
V7X_VMEM_LIMIT_BYTES = 56 * 1024 * 1024

PROJ_ROWS = 256
BF16 = jnp.bfloat16
F32 = jnp.float32


def _sigmoid(z):
    return 1.0 / (1.0 + jnp.exp(-z))


def _resident(block_shape, index_map):
    return pl.BlockSpec(block_shape, index_map, pipeline_mode=pl.Buffered(1))


def _bias_kernel(rel_ref, bucket_ref, o_ref):
    h = pl.program_id(0)
    bucket = bucket_ref[...]
    acc = jnp.zeros(bucket.shape, F32)
    for b in range(REL_BUCKETS):
        acc = jnp.where(bucket == b, rel_ref[b, h], acc)
    o_ref[...] = acc


def _band_bias(rel_bias):
    qq = jnp.arange(BLOCK)[:, None]
    kk = jnp.arange(2 * BLOCK)[None, :]
    dist = jnp.maximum(qq + BLOCK - kk, 0)
    max_exact = REL_BUCKETS // 2
    d_f = jnp.maximum(dist, 1).astype(F32)
    large = max_exact + (jnp.log(d_f / max_exact) / math.log(REL_MAX_DIST / max_exact)
                         * (REL_BUCKETS - max_exact)).astype(jnp.int32)
    large = jnp.minimum(large, REL_BUCKETS - 1)
    bucket = jnp.where(dist < max_exact, dist, large).astype(jnp.int32)
    return pl.pallas_call(
        _bias_kernel,
        grid=(N_HEADS,),
        in_specs=[pl.BlockSpec(memory_space=pltpu.SMEM),
                  pl.BlockSpec((BLOCK, 2 * BLOCK), lambda h: (0, 0))],
        out_specs=pl.BlockSpec((None, BLOCK, 2 * BLOCK), lambda h: (h, 0, 0)),
        out_shape=jax.ShapeDtypeStruct((N_HEADS, BLOCK, 2 * BLOCK), F32),
        name="band_bias",
    )(rel_bias, bucket)


def _proj_kernel(x_ref, w_ref, b_ref, o_ref):
    xb = x_ref[...].astype(BF16)
    o_ref[...] = jnp.dot(xb, w_ref[...], preferred_element_type=F32) + b_ref[...]


def _input_projection(x2, w_in, b_in, layer):
    tokens = x2.shape[0]
    return pl.pallas_call(
        _proj_kernel,
        grid=(tokens // PROJ_ROWS,),
        in_specs=[pl.BlockSpec((PROJ_ROWS, D_MODEL), lambda i: (i, 0)),
                  _resident((None, D_MODEL, IN_COLS), lambda i: (layer, 0, 0)),
                  _resident((None, 1, IN_COLS), lambda i: (layer, 0, 0))],
        out_specs=pl.BlockSpec((PROJ_ROWS, IN_COLS), lambda i: (i, 0)),
        out_shape=jax.ShapeDtypeStruct((tokens, IN_COLS), F32),
        compiler_params=pltpu.CompilerParams(
            dimension_semantics=("arbitrary",), vmem_limit_bytes=V7X_VMEM_LIMIT_BYTES),
        name="input_projection",
    )(x2, w_in, b_in)


def _mixer_kernel(sinks_ref, q_ref, ga_ref, u_ref, utail_ref, gb_ref, kv_ref, kvprev_ref,
                  bias_ref, wpool_ref, pscale_ref, o_ref, ubuf_ref):
    j = pl.program_id(1)
    is_first = j == 0

    qq = lax.broadcasted_iota(jnp.int32, (BLOCK, 2 * BLOCK), 0)
    kk = lax.broadcasted_iota(jnp.int32, (BLOCK, 2 * BLOCK), 1)
    dist = qq + BLOCK - kk
    first_key = jnp.where(is_first, BLOCK, 0)
    valid = (dist >= 0) & (dist < WINDOW) & (kk >= first_key)

    kv = jnp.concatenate([kvprev_ref[...], kv_ref[...]], axis=0)
    lane = lax.broadcasted_iota(jnp.int32, (2 * BLOCK, V7X_LANES), 1)
    low_half = lane < HEAD_DIM
    out_low_half = lax.broadcasted_iota(jnp.int32, (BLOCK, V7X_LANES), 1) < HEAD_DIM

    def head_copies(group, kv_head):
        if kv_head == 0:
            low = jnp.where(low_half, group, 0.0)
            high = pltpu.roll(low, HEAD_DIM, axis=1)
        else:
            high = jnp.where(low_half, 0.0, group)
            low = pltpu.roll(high, HEAD_DIM, axis=1)
        return low.astype(BF16), high.astype(BF16)

    for kv_head in range(N_KV_HEADS):
        k_copies = head_copies(kv[:, :V7X_LANES], kv_head)
        v_copies = head_copies(kv[:, V7X_LANES:], kv_head)
        for pair in range(KV_GROUP // 2):
            col = (kv_head * (KV_GROUP // 2) + pair) * V7X_LANES
            q_pair = (q_ref[:, col:col + V7X_LANES] * ATTN_SCALE).astype(BF16)
            pv = []
            inv = []
            for parity in range(2):
                head = kv_head * KV_GROUP + 2 * pair + parity
                s = lax.dot_general(q_pair, k_copies[parity], (((1,), (1,)), ((), ())),
                                    preferred_element_type=F32)
                s = jnp.where(valid, s + bias_ref[head], MASK_VALUE)
                sink = sinks_ref[head]
                m = jnp.maximum(jnp.max(s, axis=-1, keepdims=True), sink)
                e = jnp.exp(s - m)
                denom = jnp.sum(e, axis=-1, keepdims=True) + jnp.exp(sink - m)
                inv.append(1.0 / denom)
                pv.append(jnp.dot(e.astype(BF16), v_copies[parity],
                                  preferred_element_type=F32))
            attn = jnp.where(out_low_half, inv[0], inv[1]) * (pv[0] + pv[1])
            g = ga_ref[:, col:col + V7X_LANES]
            o_ref[:, col:col + V7X_LANES] = (attn * (g * _sigmoid(g))).astype(o_ref.dtype)

    tail_rows = lax.broadcasted_iota(jnp.int32, (POOL_TAIL, POOL_WIDTH), 0)
    tail_kept = tail_rows >= jnp.where(is_first, POOL_TAIL, 0)
    ubuf_ref[0:POOL_TAIL, :] = jnp.where(tail_kept, utail_ref[...], 0.0)
    ubuf_ref[POOL_TAIL:, :] = u_ref[...]
    pos = j * BLOCK + lax.broadcasted_iota(jnp.int32, (BLOCK, 1), 0)
    for grp, window in enumerate(POOL_WINDOWS):
        c0 = grp * POOL_GROUP_DIM
        cols = slice(c0, c0 + POOL_GROUP_DIM)
        u = ubuf_ref[POOL_TAIL:, cols]
        total = u
        for lag in range(1, window):
            total = total + ubuf_ref[POOL_TAIL - lag:POOL_TAIL - lag + BLOCK, cols]
        count = jnp.minimum(pos + 1, window).astype(F32)
        diff = total * (1.0 / count) - u
        mixed = jnp.dot(diff.astype(BF16), wpool_ref[grp], preferred_element_type=F32)
        g = gb_ref[:, cols]
        pooled = mixed * pscale_ref[:, cols] * (g * _sigmoid(g))
        o_ref[:, ATTN_WIDTH + c0:ATTN_WIDTH + c0 + POOL_GROUP_DIM] = pooled.astype(o_ref.dtype)


def _token_mixers(h, sinks, band_bias, w_pool, pool_scale, layer, batch, seq):
    nblk = seq // BLOCK
    tail_per_block = BLOCK // POOL_TAIL
    row = lambda b, j: b * nblk + j
    prev_row = lambda b, j: b * nblk + jnp.maximum(j - 1, 0)
    wide = ATTN_WIDTH
    kv_block = (Q_COLS + ATTN_WIDTH + 2 * POOL_WIDTH) // (2 * KV_COLS)
    return pl.pallas_call(
        _mixer_kernel,
        grid=(batch, nblk),
        in_specs=[
            pl.BlockSpec(memory_space=pltpu.SMEM),
            pl.BlockSpec((BLOCK, wide), lambda b, j: (row(b, j), 0)),
            pl.BlockSpec((BLOCK, wide), lambda b, j: (row(b, j), 1)),
            pl.BlockSpec((BLOCK, wide), lambda b, j: (row(b, j), 2)),
            pl.BlockSpec((POOL_TAIL, wide),
                         lambda b, j: (jnp.maximum(row(b, j) * tail_per_block - 1, 0), 2)),
            pl.BlockSpec((BLOCK, wide), lambda b, j: (row(b, j), 3)),
            pl.BlockSpec((BLOCK, 2 * KV_COLS), lambda b, j: (row(b, j), kv_block)),
            pl.BlockSpec((BLOCK, 2 * KV_COLS), lambda b, j: (prev_row(b, j), kv_block)),
            _resident((N_HEADS, BLOCK, 2 * BLOCK), lambda b, j: (0, 0, 0)),
            _resident((None, len(POOL_WINDOWS), POOL_GROUP_DIM, POOL_GROUP_DIM),
                      lambda b, j: (layer, 0, 0, 0)),
            _resident((None, 1, POOL_WIDTH), lambda b, j: (layer, 0, 0)),
        ],
        out_specs=pl.BlockSpec((BLOCK, D_MODEL), lambda b, j: (row(b, j), 0)),
        out_shape=jax.ShapeDtypeStruct((batch * seq, D_MODEL), BF16),
        scratch_shapes=[pltpu.VMEM((POOL_TAIL + BLOCK, POOL_WIDTH), F32)],
        compiler_params=pltpu.CompilerParams(
            dimension_semantics=("arbitrary", "arbitrary"),
            vmem_limit_bytes=V7X_VMEM_LIMIT_BYTES),
        name="token_mixers",
    )(sinks, h, h, h, h, h, h, h, band_bias, w_pool, pool_scale)


def _out_kernel(x_ref, c_ref, p_ref, wout_ref, wgate_ref, wple_ref, gain_ref, bias_ref, o_ref):
    x = x_ref[...]
    mix = jnp.dot(c_ref[...], wout_ref[...], preferred_element_type=F32)
    gate = _sigmoid(jnp.dot(x.astype(BF16), wgate_ref[...], preferred_element_type=F32))
    ple = gate * jnp.dot(p_ref[...].astype(BF16), wple_ref[...], preferred_element_type=F32)
    y = DEEPNORM_ALPHA * x + mix + ple
    mu = jnp.mean(y, axis=-1, keepdims=True)
    yc = y - mu
    var = jnp.mean(yc * yc, axis=-1, keepdims=True)
    o_ref[...] = yc * lax.rsqrt(var + LN_EPS) * gain_ref[...] + bias_ref[...]


def _output_stage(x2, c, p2, w_out, w_gate, w_ple, ln_gain, ln_bias, layer):
    tokens = x2.shape[0]
    rows = lambda i: (i, 0)
    per_layer = lambda i: (layer, 0, 0)
    return pl.pallas_call(
        _out_kernel,
        grid=(tokens // PROJ_ROWS,),
        in_specs=[pl.BlockSpec((PROJ_ROWS, D_MODEL), rows),
                  pl.BlockSpec((PROJ_ROWS, D_MODEL), rows),
                  pl.BlockSpec((None, PROJ_ROWS, PLE_DIM), lambda i: (layer, i, 0)),
                  _resident((None, D_MODEL, D_MODEL), per_layer),
                  _resident((None, D_MODEL, D_MODEL), per_layer),
                  _resident((None, PLE_DIM, D_MODEL), per_layer),
                  _resident((None, 1, D_MODEL), per_layer),
                  _resident((None, 1, D_MODEL), per_layer)],
        out_specs=pl.BlockSpec((PROJ_ROWS, D_MODEL), rows),
        out_shape=jax.ShapeDtypeStruct((tokens, D_MODEL), F32),
        compiler_params=pltpu.CompilerParams(
            dimension_semantics=("arbitrary",), vmem_limit_bytes=V7X_VMEM_LIMIT_BYTES),
        name="output_stage",
    )(x2, c, p2, w_out, w_gate, w_ple, ln_gain, ln_bias)


def kernel(x, p, w_in, b_in, w_out, attn_sinks, rel_bias, w_pool, pool_scale, w_ple, w_gate_ple,
           ln_gain, ln_bias):
    batch, seq, _ = x.shape
    assert x.shape == (batch, seq, D_MODEL) and seq % BLOCK == 0
    assert (batch * seq) % PROJ_ROWS == 0
    assert w_in.shape == (DEPTH, D_MODEL, IN_COLS)

    q_end, k_end, v_end = Q_COLS, Q_COLS + KV_COLS, Q_COLS + 2 * KV_COLS
    perm = jnp.concatenate([jnp.arange(0, q_end), jnp.arange(v_end, IN_COLS),
                            jnp.arange(q_end, v_end)])
    w_in_b = w_in[:, :, perm].astype(BF16)
    b_in_p = b_in[:, None, perm]
    w_out_b = w_out.astype(BF16)
    w_gate_b = w_gate_ple.astype(BF16)
    w_ple_b = w_ple.astype(BF16)
    w_pool_b = w_pool.astype(BF16)
    pool_scale3 = pool_scale[:, None, :]
    ln_gain3 = ln_gain[:, None, :]
    ln_bias3 = ln_bias[:, None, :]
    p3 = p.reshape(DEPTH, batch * seq, PLE_DIM)

    band_bias = _band_bias(rel_bias)
    x2 = x.reshape(batch * seq, D_MODEL)
    for layer in range(DEPTH):
        h = _input_projection(x2, w_in_b, b_in_p, layer)
        c = _token_mixers(h, attn_sinks[layer], band_bias, w_pool_b, pool_scale3, layer, batch, seq)
        x2 = _output_stage(x2, c, p3, w_out_b, w_gate_b, w_ple_b, ln_gain3, ln_bias3, layer)
    return x2.reshape(batch, seq, D_MODEL)
```

```python
import math

import jax
import jax.numpy as jnp
from jax import lax
from jax.experimental import pallas as pl
from jax.experimental.pallas import tpu as pltpu

D_MODEL = 2048
DEPTH = 4
PLE_DIM = 256
ATTN_WIDTH = D_MODEL // 2
POOL_WIDTH = D_MODEL - ATTN_WIDTH
HEAD_DIM = 64
N_HEADS = ATTN_WIDTH // HEAD_DIM
N_KV_HEADS = max(1, N_HEADS // 8)
KV_GROUP = N_HEADS // N_KV_HEADS
WINDOW = 128
BLOCK = WINDOW
POOL_WINDOWS = (2, 4, 8, 16)
POOL_GROUP_DIM = POOL_WIDTH // len(POOL_WINDOWS)
POOL_TAIL = 16
REL_BUCKETS = 32
REL_MAX_DIST = 128
LN_EPS = 1e-5
DEEPNORM_ALPHA = (2.0 * DEPTH) ** 0.25
Q_COLS = N_HEADS * HEAD_DIM
KV_COLS = N_KV_HEADS * HEAD_DIM
IN_COLS = Q_COLS + 2 * KV_COLS + ATTN_WIDTH + 2 * POOL_WIDTH
Q_END = Q_COLS
KV_END = Q_END + 2 * KV_COLS
GA_END = KV_END + ATTN_WIDTH
U_END = GA_END + POOL_WIDTH
ATTN_SCALE = 1.0 / math.sqrt(HEAD_DIM)
MASK_VALUE = -1e30

V7X_LANES = 128
V7X_VMEM_LIMIT_BYTES = 60 * 1024 * 1024

TILE = 2 * BLOCK
BF16 = jnp.bfloat16
F32 = jnp.float32


def _sigmoid(z):
    return 1.0 / (1.0 + jnp.exp(-z))


def _resident(block_shape, index_map):
    return pl.BlockSpec(block_shape, index_map, pipeline_mode=pl.Buffered(1))


def _bias_kernel(rel_ref, bucket_ref, o_ref):
    h = pl.program_id(0)
    bucket = bucket_ref[...]
    acc = jnp.zeros(bucket.shape, F32)
    for b in range(REL_BUCKETS):
        acc = jnp.where(bucket == b, rel_ref[b, h], acc)
    o_ref[...] = acc


def _band_bias(rel_bias):
    qq = jnp.arange(BLOCK)[:, None]
    kk = jnp.arange(2 * BLOCK)[None, :]
    dist = jnp.maximum(qq + BLOCK - kk, 0)
    max_exact = REL_BUCKETS // 2
    d_f = jnp.maximum(dist, 1).astype(F32)
    large = max_exact + (jnp.log(d_f / max_exact) / math.log(REL_MAX_DIST / max_exact)
                         * (REL_BUCKETS - max_exact)).astype(jnp.int32)
    large = jnp.minimum(large, REL_BUCKETS - 1)
    bucket = jnp.where(dist < max_exact, dist, large).astype(jnp.int32)
    return pl.pallas_call(
        _bias_kernel,
        grid=(N_HEADS,),
        in_specs=[pl.BlockSpec(memory_space=pltpu.SMEM),
                  pl.BlockSpec((BLOCK, 2 * BLOCK), lambda h: (0, 0))],
        out_specs=pl.BlockSpec((None, BLOCK, 2 * BLOCK), lambda h: (h, 0, 0)),
        out_shape=jax.ShapeDtypeStruct((N_HEADS, BLOCK, 2 * BLOCK), F32),
        name="band_bias",
    )(rel_bias, bucket)


def _attention_block(q_blk, ga_blk, kv, first_key, sinks_ref, bias_ref, c_ref, row0):
    qq = lax.broadcasted_iota(jnp.int32, (BLOCK, 2 * BLOCK), 0)
    kk = lax.broadcasted_iota(jnp.int32, (BLOCK, 2 * BLOCK), 1)
    dist = qq + BLOCK - kk
    valid = (dist >= 0) & (dist < WINDOW) & (kk >= first_key)
    low_half = lax.broadcasted_iota(jnp.int32, (2 * BLOCK, V7X_LANES), 1) < HEAD_DIM
    out_low_half = lax.broadcasted_iota(jnp.int32, (BLOCK, V7X_LANES), 1) < HEAD_DIM

    def head_copies(group, kv_head):
        if kv_head == 0:
            low = jnp.where(low_half, group, 0.0)
            high = pltpu.roll(low, HEAD_DIM, axis=1)
        else:
            high = jnp.where(low_half, 0.0, group)
            low = pltpu.roll(high, HEAD_DIM, axis=1)
        return low.astype(BF16), high.astype(BF16)

    for kv_head in range(N_KV_HEADS):
        k_copies = head_copies(kv[:, :V7X_LANES], kv_head)
        v_copies = head_copies(kv[:, V7X_LANES:], kv_head)
        for pair in range(KV_GROUP // 2):
            col = (kv_head * (KV_GROUP // 2) + pair) * V7X_LANES
            q_pair = (q_blk[:, col:col + V7X_LANES] * ATTN_SCALE).astype(BF16)
            pv = []
            inv = []
            for parity in range(2):
                head = kv_head * KV_GROUP + 2 * pair + parity
                s = lax.dot_general(q_pair, k_copies[parity], (((1,), (1,)), ((), ())),
                                    preferred_element_type=F32)
                s = jnp.where(valid, s + bias_ref[head], MASK_VALUE)
                sink = sinks_ref[head]
                m = jnp.maximum(jnp.max(s, axis=-1, keepdims=True), sink)
                e = jnp.exp(s - m)
                denom = jnp.sum(e, axis=-1, keepdims=True) + jnp.exp(sink - m)
                inv.append(1.0 / denom)
                pv.append(jnp.dot(e.astype(BF16), v_copies[parity],
                                  preferred_element_type=F32))
            attn = jnp.where(out_low_half, inv[0], inv[1]) * (pv[0] + pv[1])
            g = ga_blk[:, col:col + V7X_LANES]
            c_ref[row0:row0 + BLOCK, col:col + V7X_LANES] = (
                attn * (g * _sigmoid(g))).astype(c_ref.dtype)


def _layer_kernel(sinks_ref, x_ref, p_ref, win_ref, bin_ref, wout_ref, wgate_ref, wple_ref,
                  wpool_ref, pscale_ref, bias_ref, gain_ref, lnb_ref, o_ref,
                  kv_ref, u_ref, c_ref):
    j = pl.program_id(1)

    @pl.when(j == 0)
    def _():
        kv_ref[0:BLOCK, :] = jnp.zeros((BLOCK, 2 * KV_COLS), F32)
        u_ref[0:POOL_TAIL, :] = jnp.zeros((POOL_TAIL, POOL_WIDTH), F32)

    x = x_ref[...]
    xb = x.astype(BF16)

    def project(lo, hi):
        return (jnp.dot(xb, win_ref[:, lo:hi], preferred_element_type=F32) + bin_ref[:, lo:hi])

    q = project(0, Q_END)
    kv_ref[BLOCK:, :] = project(Q_END, KV_END)
    ga = project(KV_END, GA_END)
    u_ref[POOL_TAIL:, :] = project(GA_END, U_END)
    gb = project(U_END, IN_COLS)

    for blk in range(TILE // BLOCK):
        r0 = blk * BLOCK
        first_key = jnp.where(j == 0, BLOCK, 0) if blk == 0 else 0
        _attention_block(q[r0:r0 + BLOCK], ga[r0:r0 + BLOCK], kv_ref[r0:r0 + 2 * BLOCK, :],
                         first_key, sinks_ref, bias_ref, c_ref, r0)

    pos = j * TILE + lax.broadcasted_iota(jnp.int32, (TILE, 1), 0)
    for grp, window in enumerate(POOL_WINDOWS):
        c0 = grp * POOL_GROUP_DIM
        cols = slice(c0, c0 + POOL_GROUP_DIM)
        u = u_ref[POOL_TAIL:, cols]
        total = u
        for lag in range(1, window):
            total = total + u_ref[POOL_TAIL - lag:POOL_TAIL - lag + TILE, cols]
        count = jnp.minimum(pos + 1, window).astype(F32)
        diff = total * (1.0 / count) - u
        mixed = jnp.dot(diff.astype(BF16), wpool_ref[grp], preferred_element_type=F32)
        g = gb[:, cols]
        pooled = mixed * pscale_ref[:, cols] * (g * _sigmoid(g))
        c_ref[:, ATTN_WIDTH + c0:ATTN_WIDTH + c0 + POOL_GROUP_DIM] = pooled.astype(c_ref.dtype)

    kv_ref[0:BLOCK, :] = kv_ref[TILE:TILE + BLOCK, :]
    u_ref[0:POOL_TAIL, :] = u_ref[TILE:TILE + POOL_TAIL, :]

    mix = jnp.dot(c_ref[...], wout_ref[...], preferred_element_type=F32)
    gate = _sigmoid(jnp.dot(xb, wgate_ref[...], preferred_element_type=F32))
    ple = gate * jnp.dot(p_ref[...].astype(BF16), wple_ref[...], preferred_element_type=F32)
    y = DEEPNORM_ALPHA * x + mix + ple
    mu = jnp.mean(y, axis=-1, keepdims=True)
    yc = y - mu
    var = jnp.mean(yc * yc, axis=-1, keepdims=True)
    o_ref[...] = yc * lax.rsqrt(var + LN_EPS) * gain_ref[...] + lnb_ref[...]


def _layer(x3, p4, sinks, band_bias, w_in, b_in, w_out, w_gate, w_ple, w_pool, pool_scale,
           ln_gain, ln_bias, layer):
    batch, seq, _ = x3.shape
    tile = lambda b, j: (b, j, 0)
    per_layer = lambda b, j: (layer, 0, 0)
    return pl.pallas_call(
        _layer_kernel,
        grid=(batch, seq // TILE),
        in_specs=[
            pl.BlockSpec(memory_space=pltpu.SMEM),
            pl.BlockSpec((None, TILE, D_MODEL), tile),
            pl.BlockSpec((None, None, TILE, PLE_DIM), lambda b, j: (layer, b, j, 0)),
            _resident((None, D_MODEL, IN_COLS), per_layer),
            _resident((None, 1, IN_COLS), per_layer),
            _resident((None, D_MODEL, D_MODEL), per_layer),
            _resident((None, D_MODEL, D_MODEL), per_layer),
            _resident((None, PLE_DIM, D_MODEL), per_layer),
            _resident((None, len(POOL_WINDOWS), POOL_GROUP_DIM, POOL_GROUP_DIM),
                      lambda b, j: (layer, 0, 0, 0)),
            _resident((None, 1, POOL_WIDTH), per_layer),
            _resident((N_HEADS, BLOCK, 2 * BLOCK), lambda b, j: (0, 0, 0)),
            _resident((None, 1, D_MODEL), per_layer),
            _resident((None, 1, D_MODEL), per_layer),
        ],
        out_specs=pl.BlockSpec((None, TILE, D_MODEL), tile),
        out_shape=jax.ShapeDtypeStruct((batch, seq, D_MODEL), F32),
        scratch_shapes=[pltpu.VMEM((BLOCK + TILE, 2 * KV_COLS), F32),
                        pltpu.VMEM((POOL_TAIL + TILE, POOL_WIDTH), F32),
                        pltpu.VMEM((TILE, D_MODEL), BF16)],
        compiler_params=pltpu.CompilerParams(
            dimension_semantics=("arbitrary", "arbitrary"),
            vmem_limit_bytes=V7X_VMEM_LIMIT_BYTES),
        name="hybrid_layer",
    )(sinks, x3, p4, w_in, b_in, w_out, w_gate, w_ple, w_pool, pool_scale, band_bias,
      ln_gain, ln_bias)


def kernel(x, p, w_in, b_in, w_out, attn_sinks, rel_bias, w_pool, pool_scale, w_ple, w_gate_ple,
           ln_gain, ln_bias):
    batch, seq, _ = x.shape
    assert x.shape == (batch, seq, D_MODEL) and seq % TILE == 0
    assert w_in.shape == (DEPTH, D_MODEL, IN_COLS)

    w_in_b = w_in.astype(BF16)
    w_out_b = w_out.astype(BF16)
    w_gate_b = w_gate_ple.astype(BF16)
    w_ple_b = w_ple.astype(BF16)
    w_pool_b = w_pool.astype(BF16)
    b_in3 = b_in[:, None, :]
    pool_scale3 = pool_scale[:, None, :]
    ln_gain3 = ln_gain[:, None, :]
    ln_bias3 = ln_bias[:, None, :]

    band_bias = _band_bias(rel_bias)
    for layer in range(DEPTH):
        x = _layer(x, p, attn_sinks[layer], band_bias, w_in_b, b_in3, w_out_b, w_gate_b, w_ple_b,
                   w_pool_b, pool_scale3, ln_gain3, ln_bias3, layer)
    return x
```

```python
import math

import jax
import jax.numpy as jnp
from jax import lax
from jax.experimental import pallas as pl
from jax.experimental.pallas import tpu as pltpu

D_MODEL = 2048
DEPTH = 4
PLE_DIM = 256
ATTN_WIDTH = D_MODEL // 2
POOL_WIDTH = D_MODEL - ATTN_WIDTH
HEAD_DIM = 64
N_HEADS = ATTN_WIDTH // HEAD_DIM
N_KV_HEADS = max(1, N_HEADS // 8)
KV_GROUP = N_HEADS // N_KV_HEADS
WINDOW = 128
BLOCK = WINDOW
POOL_WINDOWS = (2, 4, 8, 16)
POOL_GROUP_DIM = POOL_WIDTH // len(POOL_WINDOWS)
POOL_TAIL = 16
REL_BUCKETS = 32
REL_MAX_DIST = 128
LN_EPS = 1e-5
DEEPNORM_ALPHA = (2.0 * DEPTH) ** 0.25
Q_COLS = N_HEADS * HEAD_DIM
KV_COLS = N_KV_HEADS * HEAD_DIM
IN_COLS = Q_COLS + 2 * KV_COLS + ATTN_WIDTH + 2 * POOL_WIDTH
Q_END = Q_COLS
KV_END = Q_END + 2 * KV_COLS
GA_END = KV_END + ATTN_WIDTH
U_END = GA_END + POOL_WIDTH
ATTN_SCALE = 1.0 / math.sqrt(HEAD_DIM)
MASK_VALUE = -1e30

V7X_LANES = 128
V7X_VMEM_LIMIT_BYTES = 60 * 1024 * 1024

TILE = 2 * BLOCK
BLOCKS_PER_TILE = TILE // BLOCK
assert 2 * HEAD_DIM == V7X_LANES
PAIRS = KV_GROUP // 2
GROUP_COLS = KV_GROUP * HEAD_DIM
ATTN_GROUPS = BLOCKS_PER_TILE * N_KV_HEADS
EMB_COLS = D_MODEL // ATTN_GROUPS
BF16 = jnp.bfloat16
F32 = jnp.float32


def _sigmoid(z):
    return 1.0 / (1.0 + jnp.exp(-z))


def _resident(block_shape, index_map):
    return pl.BlockSpec(block_shape, index_map, pipeline_mode=pl.Buffered(1))


def _group_order():
    return [kv * KV_GROUP + 2 * pair + parity
            for kv in range(N_KV_HEADS) for parity in range(2) for pair in range(PAIRS)]


def _bias_kernel(rel_ref, bucket_ref, o_ref):
    h = pl.program_id(0)
    bucket = bucket_ref[...]
    acc = jnp.zeros(bucket.shape, F32)
    for b in range(REL_BUCKETS):
        acc = jnp.where(bucket == b, rel_ref[b, h], acc)
    o_ref[...] = acc


def _band_bias(rel_bias):
    qq = jnp.arange(BLOCK)[:, None]
    kk = jnp.arange(2 * BLOCK)[None, :]
    dist = jnp.maximum(qq + BLOCK - kk, 0)
    max_exact = REL_BUCKETS // 2
    d_f = jnp.maximum(dist, 1).astype(F32)
    large = max_exact + (jnp.log(d_f / max_exact) / math.log(REL_MAX_DIST / max_exact)
                         * (REL_BUCKETS - max_exact)).astype(jnp.int32)
    large = jnp.minimum(large, REL_BUCKETS - 1)
    bucket = jnp.where(dist < max_exact, dist, large).astype(jnp.int32)
    return pl.pallas_call(
        _bias_kernel,
        grid=(N_HEADS,),
        in_specs=[pl.BlockSpec(memory_space=pltpu.SMEM),
                  pl.BlockSpec((BLOCK, 2 * BLOCK), lambda h: (0, 0))],
        out_specs=pl.BlockSpec((None, BLOCK, 2 * BLOCK), lambda h: (h, 0, 0)),
        out_shape=jax.ShapeDtypeStruct((N_HEADS, BLOCK, 2 * BLOCK), F32),
        name="band_bias",
    )(rel_bias, bucket)


def _head_copies(group, kv_head):
    low_half = lax.broadcasted_iota(jnp.int32, group.shape, 1) < HEAD_DIM
    if kv_head == 0:
        low = jnp.where(low_half, group, 0.0)
        high = pltpu.roll(low, HEAD_DIM, axis=1)
    else:
        high = jnp.where(low_half, 0.0, group)
        low = pltpu.roll(high, HEAD_DIM, axis=1)
    return low, high


def _group_scores(q_group, k_group, kv_head):
    q_rows = jnp.concatenate([q_group[:, p * V7X_LANES:(p + 1) * V7X_LANES]
                              for p in range(PAIRS)], axis=0)
    contract_lanes = (((1,), (1,)), ((), ()))
    scores = [lax.dot_general(q_rows, k.astype(BF16), contract_lanes, preferred_element_type=F32)
              .reshape(PAIRS, BLOCK, 2 * BLOCK) for k in _head_copies(k_group, kv_head)]
    return jnp.concatenate(scores, axis=0)


def _group_outputs(scores, gate, v_group, kv_head, first_key, sinks, bias):
    qq = lax.broadcasted_iota(jnp.int32, (BLOCK, 2 * BLOCK), 0)
    kk = lax.broadcasted_iota(jnp.int32, (BLOCK, 2 * BLOCK), 1)
    dist = qq + BLOCK - kk
    valid = (dist >= 0) & (dist < WINDOW) & (kk >= first_key)
    s = jnp.where(valid[None], scores + bias, MASK_VALUE)
    sink = sinks[:, :, :1]
    m = jnp.maximum(jnp.max(s, axis=-1, keepdims=True), sink)
    e = jnp.exp(s - m).astype(BF16)
    sink_term = jnp.exp(sink - m)

    lane = lax.broadcasted_iota(jnp.int32, (2 * BLOCK, V7X_LANES), 1)
    ones = [jnp.where(lane < HEAD_DIM, 1.0, 0.0), jnp.where(lane < HEAD_DIM, 0.0, 1.0)]
    rows = PAIRS * BLOCK
    acc = None
    for parity, v in enumerate(_head_copies(v_group, kv_head)):
        v_wide = jnp.concatenate([v, ones[parity]], axis=1).astype(BF16)
        part = jnp.dot(e[parity * PAIRS:(parity + 1) * PAIRS].reshape(rows, 2 * BLOCK), v_wide,
                       preferred_element_type=F32)
        acc = part if acc is None else acc + part
    low_half = lax.broadcasted_iota(jnp.int32, (rows, V7X_LANES), 1) < HEAD_DIM
    denom = acc[:, V7X_LANES:] + jnp.where(low_half, sink_term[:PAIRS].reshape(rows, 1),
                                           sink_term[PAIRS:].reshape(rows, 1))
    attn = acc[:, :V7X_LANES] / denom
    attn = jnp.concatenate([attn[p * BLOCK:(p + 1) * BLOCK] for p in range(PAIRS)], axis=1)
    return (attn * (gate * _sigmoid(gate))).astype(BF16)


def _layer_kernel(x_ref, p_ref, win_ref, bin_ref, wout_ref, wgate_ref, wple_ref,
                  wpool_ref, pscale_ref, bias_ref, sinks_ref, gain_ref, lnb_ref, o_ref,
                  kv_ref, u_ref, c_ref):
    j = pl.program_id(1)

    @pl.when(j == 0)
    def _():
        kv_ref[0:BLOCK, :] = jnp.zeros((BLOCK, 2 * KV_COLS), F32)
        u_ref[0:POOL_TAIL, :] = jnp.zeros((POOL_TAIL, POOL_WIDTH), F32)

    xb = x_ref[...].astype(BF16)
    pb = p_ref[...].astype(BF16)

    def project(lo, hi):
        return (jnp.dot(xb, win_ref[:, lo:hi], preferred_element_type=F32) + bin_ref[:, lo:hi])

    q = (project(0, Q_END) * ATTN_SCALE).astype(BF16)
    kv_ref[BLOCK:, :] = project(Q_END, KV_END)
    ga = project(KV_END, GA_END)
    u_ref[POOL_TAIL:, :] = project(GA_END, U_END)
    gb = project(U_END, IN_COLS)

    ple_chunks = []
    for blk in range(BLOCKS_PER_TILE):
        rows = slice(blk * BLOCK, (blk + 1) * BLOCK)
        first_key = jnp.where(j == 0, BLOCK, 0) if blk == 0 else 0
        kv = kv_ref[blk * BLOCK:(blk + 2) * BLOCK, :]
        for kv_head in range(N_KV_HEADS):
            cols = slice(kv_head * GROUP_COLS, (kv_head + 1) * GROUP_COLS)
            heads = slice(kv_head * KV_GROUP, (kv_head + 1) * KV_GROUP)
            scores = _group_scores(q[rows, cols], kv[:, :V7X_LANES], kv_head)
            ec = slice(len(ple_chunks) * EMB_COLS, (len(ple_chunks) + 1) * EMB_COLS)
            gate_pre = jnp.dot(xb, wgate_ref[:, ec], preferred_element_type=F32)
            emb = jnp.dot(pb, wple_ref[:, ec], preferred_element_type=F32)
            c_ref[rows, cols] = _group_outputs(scores, ga[rows, cols], kv[:, V7X_LANES:], kv_head,
                                               first_key, sinks_ref[heads], bias_ref[heads])
            ple_chunks.append(_sigmoid(gate_pre) * emb)

    pos = j * TILE + lax.broadcasted_iota(jnp.int32, (TILE, 1), 0)
    for grp, window in enumerate(POOL_WINDOWS):
        c0 = grp * POOL_GROUP_DIM
        cols = slice(c0, c0 + POOL_GROUP_DIM)
        u = u_ref[POOL_TAIL:, cols]
        total = u
        for lag in range(1, window):
            total = total + u_ref[POOL_TAIL - lag:POOL_TAIL - lag + TILE, cols]
        count = jnp.minimum(pos + 1, window).astype(F32)
        diff = total * (1.0 / count) - u
        mixed = jnp.dot(diff.astype(BF16), wpool_ref[grp], preferred_element_type=F32)
        g = gb[:, cols]
        pooled = mixed * pscale_ref[:, cols] * (g * _sigmoid(g))
        c_ref[:, ATTN_WIDTH + c0:ATTN_WIDTH + c0 + POOL_GROUP_DIM] = pooled.astype(c_ref.dtype)

    kv_ref[0:BLOCK, :] = kv_ref[TILE:TILE + BLOCK, :]
    u_ref[0:POOL_TAIL, :] = u_ref[TILE:TILE + POOL_TAIL, :]

    mix = jnp.dot(c_ref[...], wout_ref[...], preferred_element_type=F32)
    ple = jnp.concatenate(ple_chunks, axis=1)
    y = DEEPNORM_ALPHA * x_ref[...] + mix + ple
    mu = jnp.mean(y, axis=-1, keepdims=True)
    yc = y - mu
    var = jnp.mean(yc * yc, axis=-1, keepdims=True)
    o_ref[...] = yc * lax.rsqrt(var + LN_EPS) * gain_ref[...] + lnb_ref[...]


def _layer(x3, p4, sinks, band_bias, w_in, b_in, w_out, w_gate, w_ple, w_pool, pool_scale,
           ln_gain, ln_bias, layer):
    batch, seq, _ = x3.shape
    tile = lambda b, j: (b, j, 0)
    per_layer = lambda b, j: (layer, 0, 0)
    per_layer4 = lambda b, j: (layer, 0, 0, 0)
    return pl.pallas_call(
        _layer_kernel,
        grid=(batch, seq // TILE),
        in_specs=[
            pl.BlockSpec((None, TILE, D_MODEL), tile),
            pl.BlockSpec((None, None, TILE, PLE_DIM), lambda b, j: (layer, b, j, 0)),
            _resident((None, D_MODEL, IN_COLS), per_layer),
            _resident((None, 1, IN_COLS), per_layer),
            _resident((None, D_MODEL, D_MODEL), per_layer),
            _resident((None, D_MODEL, D_MODEL), per_layer),
            _resident((None, PLE_DIM, D_MODEL), per_layer),
            _resident((None, len(POOL_WINDOWS), POOL_GROUP_DIM, POOL_GROUP_DIM), per_layer4),
            _resident((None, 1, POOL_WIDTH), per_layer),
            _resident((N_HEADS, BLOCK, 2 * BLOCK), lambda b, j: (0, 0, 0)),
            _resident((None, N_HEADS, 1, V7X_LANES), per_layer4),
            _resident((None, 1, D_MODEL), per_layer),
            _resident((None, 1, D_MODEL), per_layer),
        ],
        out_specs=pl.BlockSpec((None, TILE, D_MODEL), tile),
        out_shape=jax.ShapeDtypeStruct((batch, seq, D_MODEL), F32),
        scratch_shapes=[pltpu.VMEM((BLOCK + TILE, 2 * KV_COLS), F32),
                        pltpu.VMEM((POOL_TAIL + TILE, POOL_WIDTH), F32),
                        pltpu.VMEM((TILE, D_MODEL), BF16)],
        compiler_params=pltpu.CompilerParams(
            dimension_semantics=("arbitrary", "arbitrary"),
            vmem_limit_bytes=V7X_VMEM_LIMIT_BYTES),
        name="hybrid_layer",
    )(x3, p4, w_in, b_in, w_out, w_gate, w_ple, w_pool, pool_scale, band_bias, sinks,
      ln_gain, ln_bias)


def kernel(x, p, w_in, b_in, w_out, attn_sinks, rel_bias, w_pool, pool_scale, w_ple, w_gate_ple,
           ln_gain, ln_bias):
    batch, seq, _ = x.shape
    assert x.shape == (batch, seq, D_MODEL) and seq % TILE == 0
    assert w_in.shape == (DEPTH, D_MODEL, IN_COLS)

    w_in_b = w_in.astype(BF16)
    w_out_b = w_out.astype(BF16)
    w_gate_b = w_gate_ple.astype(BF16)
    w_ple_b = w_ple.astype(BF16)
    w_pool_b = w_pool.astype(BF16)
    b_in3 = b_in[:, None, :]
    pool_scale3 = pool_scale[:, None, :]
    ln_gain3 = ln_gain[:, None, :]
    ln_bias3 = ln_bias[:, None, :]

    order = jnp.array(_group_order(), jnp.int32)
    band_bias = _band_bias(rel_bias)[order]
    sinks = jnp.broadcast_to(attn_sinks[:, order, None, None], (DEPTH, N_HEADS, 1, V7X_LANES))
    for layer in range(DEPTH):
        x = _layer(x, p, sinks, band_bias, w_in_b, b_in3, w_out_b, w_gate_b, w_ple_b,
                   w_pool_b, pool_scale3, ln_gain3, ln_bias3, layer)
    return x
```

```python
import math

import jax
import jax.numpy as jnp
from jax import lax
from jax.experimental import pallas as pl
from jax.experimental.pallas import tpu as pltpu

D_MODEL = 2048
DEPTH = 4
PLE_DIM = 256
ATTN_WIDTH = D_MODEL // 2
POOL_WIDTH = D_MODEL - ATTN_WIDTH
HEAD_DIM = 64
N_HEADS = ATTN_WIDTH // HEAD_DIM
N_KV_HEADS = max(1, N_HEADS // 8)
KV_GROUP = N_HEADS // N_KV_HEADS
WINDOW = 128
BLOCK = WINDOW
POOL_WINDOWS = (2, 4, 8, 16)
POOL_GROUP_DIM = POOL_WIDTH // len(POOL_WINDOWS)
POOL_TAIL = 16
REL_BUCKETS = 32
REL_MAX_DIST = 128
LN_EPS = 1e-5
DEEPNORM_ALPHA = (2.0 * DEPTH) ** 0.25
Q_COLS = N_HEADS * HEAD_DIM
KV_COLS = N_KV_HEADS * HEAD_DIM
IN_COLS = Q_COLS + 2 * KV_COLS + ATTN_WIDTH + 2 * POOL_WIDTH
Q_END = Q_COLS
KV_END = Q_END + 2 * KV_COLS
GA_END = KV_END + ATTN_WIDTH
U_END = GA_END + POOL_WIDTH
ATTN_SCALE = 1.0 / math.sqrt(HEAD_DIM)
MASK_VALUE = -1e30

V7X_LANES = 128
V7X_VMEM_LIMIT_BYTES = 60 * 1024 * 1024

TILE = 2 * BLOCK
BLOCKS_PER_TILE = TILE // BLOCK
assert 2 * HEAD_DIM == V7X_LANES
PAIRS = KV_GROUP // 2
GROUP_COLS = KV_GROUP * HEAD_DIM
ATTN_GROUPS = BLOCKS_PER_TILE * N_KV_HEADS
EMB_COLS = D_MODEL // ATTN_GROUPS
BF16 = jnp.bfloat16
F32 = jnp.float32


def _sigmoid(z):
    return 1.0 / (1.0 + jnp.exp(-z))


def _resident(block_shape, index_map):
    return pl.BlockSpec(block_shape, index_map, pipeline_mode=pl.Buffered(1))


def _group_order():
    return [kv * KV_GROUP + 2 * pair + parity
            for kv in range(N_KV_HEADS) for parity in range(2) for pair in range(PAIRS)]


def _bias_kernel(rel_ref, bucket_ref, o_ref):
    h = pl.program_id(0)
    bucket = bucket_ref[...]
    acc = jnp.zeros(bucket.shape, F32)
    for b in range(REL_BUCKETS):
        acc = jnp.where(bucket == b, rel_ref[b, h], acc)
    o_ref[...] = acc


def _band_bias(rel_bias):
    qq = jnp.arange(BLOCK)[:, None]
    kk = jnp.arange(2 * BLOCK)[None, :]
    dist = jnp.maximum(qq + BLOCK - kk, 0)
    max_exact = REL_BUCKETS // 2
    d_f = jnp.maximum(dist, 1).astype(F32)
    large = max_exact + (jnp.log(d_f / max_exact) / math.log(REL_MAX_DIST / max_exact)
                         * (REL_BUCKETS - max_exact)).astype(jnp.int32)
    large = jnp.minimum(large, REL_BUCKETS - 1)
    bucket = jnp.where(dist < max_exact, dist, large).astype(jnp.int32)
    return pl.pallas_call(
        _bias_kernel,
        grid=(N_HEADS,),
        in_specs=[pl.BlockSpec(memory_space=pltpu.SMEM),
                  pl.BlockSpec((BLOCK, 2 * BLOCK), lambda h: (0, 0))],
        out_specs=pl.BlockSpec((None, BLOCK, 2 * BLOCK), lambda h: (h, 0, 0)),
        out_shape=jax.ShapeDtypeStruct((N_HEADS, BLOCK, 2 * BLOCK), F32),
        name="band_bias",
    )(rel_bias, bucket)


def _head_copies(group, kv_head):
    low_half = lax.broadcasted_iota(jnp.int32, group.shape, 1) < HEAD_DIM
    if kv_head == 0:
        low = jnp.where(low_half, group, 0.0)
        high = pltpu.roll(low, HEAD_DIM, axis=1)
    else:
        high = jnp.where(low_half, 0.0, group)
        low = pltpu.roll(high, HEAD_DIM, axis=1)
    return low, high


def _group_scores(q_group, k_group, kv_head):
    q_rows = jnp.concatenate([q_group[:, p * V7X_LANES:(p + 1) * V7X_LANES]
                              for p in range(PAIRS)], axis=0)
    contract_lanes = (((1,), (1,)), ((), ()))
    scores = [lax.dot_general(q_rows, k.astype(BF16), contract_lanes, preferred_element_type=F32)
              .reshape(PAIRS, BLOCK, 2 * BLOCK) for k in _head_copies(k_group, kv_head)]
    return jnp.concatenate(scores, axis=0)


def _group_outputs(scores, gate, v_group, kv_head, first_key, sinks, bias):
    qq = lax.broadcasted_iota(jnp.int32, (BLOCK, 2 * BLOCK), 0)
    kk = lax.broadcasted_iota(jnp.int32, (BLOCK, 2 * BLOCK), 1)
    dist = qq + BLOCK - kk
    valid = (dist >= 0) & (dist < WINDOW) & (kk >= first_key)
    s = jnp.where(valid[None], scores + bias, MASK_VALUE)
    sink = sinks[:, :, :1]
    m = jnp.maximum(jnp.max(s, axis=-1, keepdims=True), sink)
    e = jnp.exp(s - m).astype(BF16)
    sink_term = jnp.exp(sink - m)

    lane = lax.broadcasted_iota(jnp.int32, (2 * BLOCK, V7X_LANES), 1)
    ones = [jnp.where(lane < HEAD_DIM, 1.0, 0.0), jnp.where(lane < HEAD_DIM, 0.0, 1.0)]
    rows = PAIRS * BLOCK
    acc = None
    for parity, v in enumerate(_head_copies(v_group, kv_head)):
        v_wide = jnp.concatenate([v, ones[parity]], axis=1).astype(BF16)
        part = jnp.dot(e[parity * PAIRS:(parity + 1) * PAIRS].reshape(rows, 2 * BLOCK), v_wide,
                       preferred_element_type=F32)
        acc = part if acc is None else acc + part
    low_half = lax.broadcasted_iota(jnp.int32, (rows, V7X_LANES), 1) < HEAD_DIM
    denom = acc[:, V7X_LANES:] + jnp.where(low_half, sink_term[:PAIRS].reshape(rows, 1),
                                           sink_term[PAIRS:].reshape(rows, 1))
    attn = acc[:, :V7X_LANES] / denom
    attn = jnp.concatenate([attn[p * BLOCK:(p + 1) * BLOCK] for p in range(PAIRS)], axis=1)
    return (attn * (gate * _sigmoid(gate))).astype(BF16)


def _layer_kernel(x_ref, p_ref, win_ref, bin_ref, wout_ref, wgate_ref, wple_ref,
                  wpool_ref, pscale_ref, bias_ref, sinks_ref, gain_ref, lnb_ref, o_ref,
                  kv_ref, u_ref, c_ref):
    j = pl.program_id(1)

    @pl.when(j == 0)
    def _():
        kv_ref[0:BLOCK, :] = jnp.zeros((BLOCK, 2 * KV_COLS), F32)
        u_ref[0:POOL_TAIL, :] = jnp.zeros((POOL_TAIL, POOL_WIDTH), F32)

    xb = x_ref[...].astype(BF16)
    pb = p_ref[...].astype(BF16)

    def project(lo, hi):
        return (jnp.dot(xb, win_ref[:, lo:hi], preferred_element_type=F32) + bin_ref[:, lo:hi])

    q = (project(0, Q_END) * ATTN_SCALE).astype(BF16)
    kv_ref[BLOCK:, :] = project(Q_END, KV_END)
    ga = project(KV_END, GA_END)
    u_ref[POOL_TAIL:, :] = project(GA_END, U_END)
    gb = project(U_END, IN_COLS)

    groups = []
    for blk in range(BLOCKS_PER_TILE):
        rows = slice(blk * BLOCK, (blk + 1) * BLOCK)
        first_key = jnp.where(j == 0, BLOCK, 0) if blk == 0 else 0
        kv = kv_ref[blk * BLOCK:(blk + 2) * BLOCK, :]
        for kv_head in range(N_KV_HEADS):
            cols = slice(kv_head * GROUP_COLS, (kv_head + 1) * GROUP_COLS)
            scores = _group_scores(q[rows, cols], kv[:, :V7X_LANES], kv_head)
            groups.append((rows, cols, kv_head, first_key, kv[:, V7X_LANES:], scores))
    ple_chunks = []
    for n, (rows, cols, kv_head, first_key, v_group, scores) in enumerate(groups):
        heads = slice(kv_head * KV_GROUP, (kv_head + 1) * KV_GROUP)
        ec = slice(n * EMB_COLS, (n + 1) * EMB_COLS)
        gate_pre = jnp.dot(xb, wgate_ref[:, ec], preferred_element_type=F32)
        emb = jnp.dot(pb, wple_ref[:, ec], preferred_element_type=F32)
        c_ref[rows, cols] = _group_outputs(scores, ga[rows, cols], v_group, kv_head,
                                           first_key, sinks_ref[heads], bias_ref[heads])
        ple_chunks.append(_sigmoid(gate_pre) * emb)

    pos = j * TILE + lax.broadcasted_iota(jnp.int32, (TILE, 1), 0)
    for grp, window in enumerate(POOL_WINDOWS):
        c0 = grp * POOL_GROUP_DIM
        cols = slice(c0, c0 + POOL_GROUP_DIM)
        u = u_ref[POOL_TAIL:, cols]
        total = u
        for lag in range(1, window):
            total = total + u_ref[POOL_TAIL - lag:POOL_TAIL - lag + TILE, cols]
        count = jnp.minimum(pos + 1, window).astype(F32)
        diff = total * (1.0 / count) - u
        mixed = jnp.dot(diff.astype(BF16), wpool_ref[grp], preferred_element_type=F32)
        g = gb[:, cols]
        pooled = mixed * pscale_ref[:, cols] * (g * _sigmoid(g))
        c_ref[:, ATTN_WIDTH + c0:ATTN_WIDTH + c0 + POOL_GROUP_DIM] = pooled.astype(c_ref.dtype)

    kv_ref[0:BLOCK, :] = kv_ref[TILE:TILE + BLOCK, :]
    u_ref[0:POOL_TAIL, :] = u_ref[TILE:TILE + POOL_TAIL, :]

    mix = jnp.dot(c_ref[...], wout_ref[...], preferred_element_type=F32)
    ple = jnp.concatenate(ple_chunks, axis=1)
    y = DEEPNORM_ALPHA * x_ref[...] + mix + ple
    mu = jnp.mean(y, axis=-1, keepdims=True)
    yc = y - mu
    var = jnp.mean(yc * yc, axis=-1, keepdims=True)
    o_ref[...] = yc * lax.rsqrt(var + LN_EPS) * gain_ref[...] + lnb_ref[...]


def _layer(x3, p4, sinks, band_bias, w_in, b_in, w_out, w_gate, w_ple, w_pool, pool_scale,
           ln_gain, ln_bias, layer):
    batch, seq, _ = x3.shape
    tile = lambda b, j: (b, j, 0)
    per_layer = lambda b, j: (layer, 0, 0)
    per_layer4 = lambda b, j: (layer, 0, 0, 0)
    return pl.pallas_call(
        _layer_kernel,
        grid=(batch, seq // TILE),
        in_specs=[
            pl.BlockSpec((None, TILE, D_MODEL), tile),
            pl.BlockSpec((None, None, TILE, PLE_DIM), lambda b, j: (layer, b, j, 0)),
            _resident((None, D_MODEL, IN_COLS), per_layer),
            _resident((None, 1, IN_COLS), per_layer),
            _resident((None, D_MODEL, D_MODEL), per_layer),
            _resident((None, D_MODEL, D_MODEL), per_layer),
            _resident((None, PLE_DIM, D_MODEL), per_layer),
            _resident((None, len(POOL_WINDOWS), POOL_GROUP_DIM, POOL_GROUP_DIM), per_layer4),
            _resident((None, 1, POOL_WIDTH), per_layer),
            _resident((N_HEADS, BLOCK, 2 * BLOCK), lambda b, j: (0, 0, 0)),
            _resident((None, N_HEADS, 1, V7X_LANES), per_layer4),
            _resident((None, 1, D_MODEL), per_layer),
            _resident((None, 1, D_MODEL), per_layer),
        ],
        out_specs=pl.BlockSpec((None, TILE, D_MODEL), tile),
        out_shape=jax.ShapeDtypeStruct((batch, seq, D_MODEL), F32),
        scratch_shapes=[pltpu.VMEM((BLOCK + TILE, 2 * KV_COLS), F32),
                        pltpu.VMEM((POOL_TAIL + TILE, POOL_WIDTH), F32),
                        pltpu.VMEM((TILE, D_MODEL), BF16)],
        compiler_params=pltpu.CompilerParams(
            dimension_semantics=("arbitrary", "arbitrary"),
            vmem_limit_bytes=V7X_VMEM_LIMIT_BYTES),
        name="hybrid_layer",
    )(x3, p4, w_in, b_in, w_out, w_gate, w_ple, w_pool, pool_scale, band_bias, sinks,
      ln_gain, ln_bias)


def kernel(x, p, w_in, b_in, w_out, attn_sinks, rel_bias, w_pool, pool_scale, w_ple, w_gate_ple,
           ln_gain, ln_bias):
    batch, seq, _ = x.shape
    assert x.shape == (batch, seq, D_MODEL) and seq % TILE == 0
    assert w_in.shape == (DEPTH, D_MODEL, IN_COLS)

    w_in_b = w_in.astype(BF16)
    w_out_b = w_out.astype(BF16)
    w_gate_b = w_gate_ple.astype(BF16)
    w_ple_b = w_ple.astype(BF16)
    w_pool_b = w_pool.astype(BF16)
    b_in3 = b_in[:, None, :]
    pool_scale3 = pool_scale[:, None, :]
    ln_gain3 = ln_gain[:, None, :]
    ln_bias3 = ln_bias[:, None, :]

    order = jnp.array(_group_order(), jnp.int32)
    band_bias = _band_bias(rel_bias)[order]
    sinks = jnp.broadcast_to(attn_sinks[:, order, None, None], (DEPTH, N_HEADS, 1, V7X_LANES))
    for layer in range(DEPTH):
        x = _layer(x, p, sinks, band_bias, w_in_b, b_in3, w_out_b, w_gate_b, w_ple_b,
                   w_pool_b, pool_scale3, ln_gain3, ln_bias3, layer)
    return x
```

```python
import math

import jax
import jax.numpy as jnp
from jax import lax
from jax.experimental import pallas as pl
from jax.experimental.pallas import tpu as pltpu

D_MODEL = 2048
DEPTH = 4
PLE_DIM = 256
ATTN_WIDTH = D_MODEL // 2
POOL_WIDTH = D_MODEL - ATTN_WIDTH
HEAD_DIM = 64
N_HEADS = ATTN_WIDTH // HEAD_DIM
N_KV_HEADS = max(1, N_HEADS // 8)
KV_GROUP = N_HEADS // N_KV_HEADS
WINDOW = 128
BLOCK = WINDOW
POOL_WINDOWS = (2, 4, 8, 16)
POOL_GROUP_DIM = POOL_WIDTH // len(POOL_WINDOWS)
POOL_TAIL = 16
REL_BUCKETS = 32
REL_MAX_DIST = 128
LN_EPS = 1e-5
DEEPNORM_ALPHA = (2.0 * DEPTH) ** 0.25
Q_COLS = N_HEADS * HEAD_DIM
KV_COLS = N_KV_HEADS * HEAD_DIM
IN_COLS = Q_COLS + 2 * KV_COLS + ATTN_WIDTH + 2 * POOL_WIDTH
Q_END = Q_COLS
KV_END = Q_END + 2 * KV_COLS
GA_END = KV_END + ATTN_WIDTH
U_END = GA_END + POOL_WIDTH
ATTN_SCALE = 1.0 / math.sqrt(HEAD_DIM)
MASK_VALUE = -1e30

V7X_LANES = 128
PADDED_COLS = D_MODEL + V7X_LANES
V7X_VMEM_LIMIT_BYTES = 60 * 1024 * 1024

TILE = 2 * BLOCK
BLOCKS_PER_TILE = TILE // BLOCK
assert 2 * HEAD_DIM == V7X_LANES
PAIRS = KV_GROUP // 2
GROUP_COLS = KV_GROUP * HEAD_DIM
ATTN_GROUPS = BLOCKS_PER_TILE * N_KV_HEADS
EMB_COLS = D_MODEL // ATTN_GROUPS
POOL_SLICE = 2 * POOL_WIDTH // ATTN_GROUPS
assert POOL_WIDTH % POOL_SLICE == 0
BF16 = jnp.bfloat16
F32 = jnp.float32


def _sigmoid(z):
    return 1.0 / (1.0 + jnp.exp(-z))


def _resident(block_shape, index_map):
    return pl.BlockSpec(block_shape, index_map, pipeline_mode=pl.Buffered(1))


def _group_order():
    return [kv * KV_GROUP + 2 * pair + parity
            for kv in range(N_KV_HEADS) for parity in range(2) for pair in range(PAIRS)]


def _bias_kernel(rel_ref, bucket_ref, o_ref):
    h = pl.program_id(0)
    bucket = bucket_ref[...]
    acc = jnp.zeros(bucket.shape, F32)
    for b in range(REL_BUCKETS):
        acc = jnp.where(bucket == b, rel_ref[b, h], acc)
    o_ref[...] = acc


def _band_bias(rel_bias):
    qq = jnp.arange(BLOCK)[:, None]
    kk = jnp.arange(2 * BLOCK)[None, :]
    dist = jnp.maximum(qq + BLOCK - kk, 0)
    max_exact = REL_BUCKETS // 2
    d_f = jnp.maximum(dist, 1).astype(F32)
    large = max_exact + (jnp.log(d_f / max_exact) / math.log(REL_MAX_DIST / max_exact)
                         * (REL_BUCKETS - max_exact)).astype(jnp.int32)
    large = jnp.minimum(large, REL_BUCKETS - 1)
    bucket = jnp.where(dist < max_exact, dist, large).astype(jnp.int32)
    return pl.pallas_call(
        _bias_kernel,
        grid=(N_HEADS,),
        in_specs=[pl.BlockSpec(memory_space=pltpu.SMEM),
                  pl.BlockSpec((BLOCK, 2 * BLOCK), lambda h: (0, 0))],
        out_specs=pl.BlockSpec((None, BLOCK, 2 * BLOCK), lambda h: (h, 0, 0)),
        out_shape=jax.ShapeDtypeStruct((N_HEADS, BLOCK, 2 * BLOCK), F32),
        name="band_bias",
    )(rel_bias, bucket)


def _head_copies(group, kv_head):
    low_half = lax.broadcasted_iota(jnp.int32, group.shape, 1) < HEAD_DIM
    if kv_head == 0:
        low = jnp.where(low_half, group, 0.0)
        high = pltpu.roll(low, HEAD_DIM, axis=1)
    else:
        high = jnp.where(low_half, 0.0, group)
        low = pltpu.roll(high, HEAD_DIM, axis=1)
    return low, high


def _group_scores(q_group, k_group, kv_head):
    q_rows = jnp.concatenate([q_group[:, p * V7X_LANES:(p + 1) * V7X_LANES]
                              for p in range(PAIRS)], axis=0)
    contract_lanes = (((1,), (1,)), ((), ()))
    scores = [lax.dot_general(q_rows, k.astype(BF16), contract_lanes, preferred_element_type=F32)
              .reshape(PAIRS, BLOCK, 2 * BLOCK) for k in _head_copies(k_group, kv_head)]
    return jnp.concatenate(scores, axis=0)


def _group_outputs(scores, gate, v_group, kv_head, first_key, sinks, bias):
    qq = lax.broadcasted_iota(jnp.int32, (BLOCK, 2 * BLOCK), 0)
    kk = lax.broadcasted_iota(jnp.int32, (BLOCK, 2 * BLOCK), 1)
    dist = qq + BLOCK - kk
    valid = (dist >= 0) & (dist < WINDOW) & (kk >= first_key)
    s = jnp.where(valid[None], scores + bias, MASK_VALUE)
    sink = sinks[:, :, :1]
    m = jnp.maximum(jnp.max(s, axis=-1, keepdims=True), sink)
    e = jnp.exp(s - m).astype(BF16)
    sink_term = jnp.exp(sink - m)

    lane = lax.broadcasted_iota(jnp.int32, (2 * BLOCK, V7X_LANES), 1)
    ones = [jnp.where(lane < HEAD_DIM, 1.0, 0.0), jnp.where(lane < HEAD_DIM, 0.0, 1.0)]
    rows = PAIRS * BLOCK
    acc = None
    for parity, v in enumerate(_head_copies(v_group, kv_head)):
        v_wide = jnp.concatenate([v, ones[parity]], axis=1).astype(BF16)
        part = jnp.dot(e[parity * PAIRS:(parity + 1) * PAIRS].reshape(rows, 2 * BLOCK), v_wide,
                       preferred_element_type=F32)
        acc = part if acc is None else acc + part
    low_half = lax.broadcasted_iota(jnp.int32, (rows, V7X_LANES), 1) < HEAD_DIM
    denom = acc[:, V7X_LANES:] + jnp.where(low_half, sink_term[:PAIRS].reshape(rows, 1),
                                           sink_term[PAIRS:].reshape(rows, 1))
    attn = acc[:, :V7X_LANES] / denom
    attn = jnp.concatenate([attn[p * BLOCK:(p + 1) * BLOCK] for p in range(PAIRS)], axis=1)
    return (attn * (gate * _sigmoid(gate))).astype(BF16)


def _layer_kernel(x_ref, p_ref, win_ref, bin_ref, wout_ref, wgate_ref, wple_ref,
                  wpool_ref, pscale_ref, bias_ref, sinks_ref, gain_ref, lnb_ref, o_ref,
                  kv_ref, u_ref, c_ref):
    j = pl.program_id(1)

    @pl.when(j == 0)
    def _():
        kv_ref[0:BLOCK, :] = jnp.zeros((BLOCK, 2 * KV_COLS), F32)
        u_ref[0:POOL_TAIL, :] = jnp.zeros((POOL_TAIL, POOL_WIDTH), F32)

    xb = x_ref[...].astype(BF16)
    pb = p_ref[...].astype(BF16)

    def project(lo, hi):
        return (jnp.dot(xb, win_ref[:, lo:hi], preferred_element_type=F32) + bin_ref[:, lo:hi])

    q = (project(0, Q_END) * ATTN_SCALE).astype(BF16)
    kv_ref[BLOCK:, :] = project(Q_END, KV_END)
    ga = project(KV_END, GA_END)

    groups = []
    for blk in range(BLOCKS_PER_TILE):
        rows = slice(blk * BLOCK, (blk + 1) * BLOCK)
        first_key = jnp.where(j == 0, BLOCK, 0) if blk == 0 else 0
        kv = kv_ref[blk * BLOCK:(blk + 2) * BLOCK, :]
        for kv_head in range(N_KV_HEADS):
            cols = slice(kv_head * GROUP_COLS, (kv_head + 1) * GROUP_COLS)
            scores = _group_scores(q[rows, cols], kv[:, :V7X_LANES], kv_head)
            groups.append((rows, cols, kv_head, first_key, kv[:, V7X_LANES:], scores))
    ple_chunks = []
    gb_chunks = []
    for n, (rows, cols, kv_head, first_key, v_group, scores) in enumerate(groups):
        heads = slice(kv_head * KV_GROUP, (kv_head + 1) * KV_GROUP)
        pc = (n * POOL_SLICE) % POOL_WIDTH
        if n * POOL_SLICE < POOL_WIDTH:
            u_ref[POOL_TAIL:, pc:pc + POOL_SLICE] = project(GA_END + pc, GA_END + pc + POOL_SLICE)
        else:
            gb_chunks.append(project(U_END + pc, U_END + pc + POOL_SLICE))
        ec = slice(n * EMB_COLS, (n + 1) * EMB_COLS)
        gate_pre = jnp.dot(xb, wgate_ref[:, ec], preferred_element_type=F32)
        emb = jnp.dot(pb, wple_ref[:, ec], preferred_element_type=F32)
        c_ref[rows, cols] = _group_outputs(scores, ga[rows, cols], v_group, kv_head,
                                           first_key, sinks_ref[heads], bias_ref[heads])
        ple_chunks.append(_sigmoid(gate_pre) * emb)

    gb = jnp.concatenate(gb_chunks, axis=1)
    pos = j * TILE + lax.broadcasted_iota(jnp.int32, (TILE, 1), 0)
    for grp, window in enumerate(POOL_WINDOWS):
        c0 = grp * POOL_GROUP_DIM
        cols = slice(c0, c0 + POOL_GROUP_DIM)
        total = u_ref[:, cols]
        shift = 1
        while shift < window:
            total = total + pltpu.roll(total, shift, axis=0)
            shift *= 2
        total = total[POOL_TAIL:]
        u = u_ref[POOL_TAIL:, cols]
        count = jnp.minimum(pos + 1, window).astype(F32)
        diff = total * (1.0 / count) - u
        mixed = jnp.dot(diff.astype(BF16), wpool_ref[grp], preferred_element_type=F32)
        g = gb[:, cols]
        pooled = mixed * pscale_ref[:, cols] * (g * _sigmoid(g))
        c_ref[:, ATTN_WIDTH + c0:ATTN_WIDTH + c0 + POOL_GROUP_DIM] = pooled.astype(c_ref.dtype)

    kv_ref[0:BLOCK, :] = kv_ref[TILE:TILE + BLOCK, :]
    u_ref[0:POOL_TAIL, :] = u_ref[TILE:TILE + POOL_TAIL, :]

    ple = jnp.concatenate(ple_chunks, axis=1)
    for blk in range(BLOCKS_PER_TILE):
        rows = slice(blk * BLOCK, (blk + 1) * BLOCK)
        mix = jnp.dot(c_ref[rows, :], wout_ref[:, :D_MODEL], preferred_element_type=F32)
        y = DEEPNORM_ALPHA * x_ref[rows, :] + mix + ple[rows]
        mu = jnp.mean(y, axis=-1, keepdims=True)
        yc = y - mu
        var = jnp.mean(yc * yc, axis=-1, keepdims=True)
        o_ref[rows, :] = yc * lax.rsqrt(var + LN_EPS) * gain_ref[...] + lnb_ref[...]


def _layer(x3, p4, sinks, band_bias, w_in, b_in, w_out, w_gate, w_ple, w_pool, pool_scale,
           ln_gain, ln_bias, layer):
    batch, seq, _ = x3.shape
    tile = lambda b, j: (b, j, 0)
    per_layer = lambda b, j: (layer, 0, 0)
    per_layer4 = lambda b, j: (layer, 0, 0, 0)
    return pl.pallas_call(
        _layer_kernel,
        grid=(batch, seq // TILE),
        in_specs=[
            pl.BlockSpec((None, TILE, D_MODEL), tile),
            pl.BlockSpec((None, None, TILE, PLE_DIM), lambda b, j: (layer, b, j, 0)),
            _resident((None, D_MODEL, IN_COLS), per_layer),
            _resident((None, 1, IN_COLS), per_layer),
            _resident((None, D_MODEL, PADDED_COLS), per_layer),
            _resident((None, D_MODEL, PADDED_COLS), per_layer),
            _resident((None, PLE_DIM, PADDED_COLS), per_layer),
            _resident((None, len(POOL_WINDOWS), POOL_GROUP_DIM, POOL_GROUP_DIM), per_layer4),
            _resident((None, 1, POOL_WIDTH), per_layer),
            _resident((N_HEADS, BLOCK, 2 * BLOCK), lambda b, j: (0, 0, 0)),
            _resident((None, N_HEADS, 1, V7X_LANES), per_layer4),
            _resident((None, 1, D_MODEL), per_layer),
            _resident((None, 1, D_MODEL), per_layer),
        ],
        out_specs=pl.BlockSpec((None, TILE, D_MODEL), tile),
        out_shape=jax.ShapeDtypeStruct((batch, seq, D_MODEL), F32),
        scratch_shapes=[pltpu.VMEM((BLOCK + TILE, 2 * KV_COLS), F32),
                        pltpu.VMEM((POOL_TAIL + TILE, POOL_WIDTH), F32),
                        pltpu.VMEM((TILE, D_MODEL), BF16)],
        compiler_params=pltpu.CompilerParams(
            dimension_semantics=("arbitrary", "arbitrary"),
            vmem_limit_bytes=V7X_VMEM_LIMIT_BYTES),
        name="hybrid_layer",
    )(x3, p4, w_in, b_in, w_out, w_gate, w_ple, w_pool, pool_scale, band_bias, sinks,
      ln_gain, ln_bias)


def _pad_columns(w):
    return jnp.pad(w, ((0, 0), (0, 0), (0, PADDED_COLS - D_MODEL)))


def kernel(x, p, w_in, b_in, w_out, attn_sinks, rel_bias, w_pool, pool_scale, w_ple, w_gate_ple,
           ln_gain, ln_bias):
    batch, seq, _ = x.shape
    assert x.shape == (batch, seq, D_MODEL) and seq % TILE == 0
    assert w_in.shape == (DEPTH, D_MODEL, IN_COLS)

    w_in_b = w_in.astype(BF16)
    w_out_b = _pad_columns(w_out.astype(BF16))
    w_gate_b = _pad_columns(w_gate_ple.astype(BF16))
    w_ple_b = _pad_columns(w_ple.astype(BF16))
    w_pool_b = w_pool.astype(BF16)
    b_in3 = b_in[:, None, :]
    pool_scale3 = pool_scale[:, None, :]
    ln_gain3 = ln_gain[:, None, :]
    ln_bias3 = ln_bias[:, None, :]

    order = jnp.array(_group_order(), jnp.int32)
    band_bias = _band_bias(rel_bias)[order]
    sinks = jnp.broadcast_to(attn_sinks[:, order, None, None], (DEPTH, N_HEADS, 1, V7X_LANES))
    for layer in range(DEPTH):
        x = _layer(x, p, sinks, band_bias, w_in_b, b_in3, w_out_b, w_gate_b, w_ple_b,
                   w_pool_b, pool_scale3, ln_gain3, ln_bias3, layer)
    return x
```

```python
import functools
import math

import jax
import jax.numpy as jnp
from jax import lax
from jax.experimental import pallas as pl
from jax.experimental.pallas import tpu as pltpu

D_MODEL = 2048
DEPTH = 4
PLE_DIM = 256
ATTN_WIDTH = D_MODEL // 2
POOL_WIDTH = D_MODEL - ATTN_WIDTH
HEAD_DIM = 64
N_HEADS = ATTN_WIDTH // HEAD_DIM
N_KV_HEADS = max(1, N_HEADS // 8)
KV_GROUP = N_HEADS // N_KV_HEADS
WINDOW = 128
BLOCK = WINDOW
POOL_WINDOWS = (2, 4, 8, 16)
POOL_GROUP_DIM = POOL_WIDTH // len(POOL_WINDOWS)
POOL_TAIL = 16
REL_BUCKETS = 32
REL_MAX_DIST = 128
LN_EPS = 1e-5
DEEPNORM_ALPHA = (2.0 * DEPTH) ** 0.25
Q_COLS = N_HEADS * HEAD_DIM
KV_COLS = N_KV_HEADS * HEAD_DIM
IN_COLS = Q_COLS + 2 * KV_COLS + ATTN_WIDTH + 2 * POOL_WIDTH
Q_END = Q_COLS
KV_END = Q_END + 2 * KV_COLS
GA_END = KV_END + ATTN_WIDTH
U_END = GA_END + POOL_WIDTH
ATTN_SCALE = 1.0 / math.sqrt(HEAD_DIM)
MASK_VALUE = -1e30

V7X_LANES = 128
PADDED_COLS = D_MODEL + V7X_LANES
V7X_VMEM_LIMIT_BYTES = 62 * 1024 * 1024
STAGE_ROWS = 16
STAGE_SLOTS = 8

TILE = 2 * BLOCK
BLOCKS_PER_TILE = TILE // BLOCK
assert 2 * HEAD_DIM == V7X_LANES
PAIRS = KV_GROUP // 2
GROUP_COLS = KV_GROUP * HEAD_DIM
ATTN_GROUPS = BLOCKS_PER_TILE * N_KV_HEADS
EMB_COLS = D_MODEL // ATTN_GROUPS
POOL_SLICE = 2 * POOL_WIDTH // ATTN_GROUPS
assert POOL_WIDTH % POOL_SLICE == 0
BF16 = jnp.bfloat16
F32 = jnp.float32


def _sigmoid(z):
    return 1.0 / (1.0 + jnp.exp(-z))


def _resident(block_shape, index_map):
    return pl.BlockSpec(block_shape, index_map, pipeline_mode=pl.Buffered(1))


def _group_order():
    return [kv * KV_GROUP + 2 * pair + parity
            for kv in range(N_KV_HEADS) for parity in range(2) for pair in range(PAIRS)]


def _bias_kernel(rel_ref, bucket_ref, o_ref):
    h = pl.program_id(0)
    bucket = bucket_ref[...]
    acc = jnp.zeros(bucket.shape, F32)
    for b in range(REL_BUCKETS):
        acc = jnp.where(bucket == b, rel_ref[b, h], acc)
    o_ref[...] = acc


def _band_bias(rel_bias):
    qq = jnp.arange(BLOCK)[:, None]
    kk = jnp.arange(2 * BLOCK)[None, :]
    dist = jnp.maximum(qq + BLOCK - kk, 0)
    max_exact = REL_BUCKETS // 2
    d_f = jnp.maximum(dist, 1).astype(F32)
    large = max_exact + (jnp.log(d_f / max_exact) / math.log(REL_MAX_DIST / max_exact)
                         * (REL_BUCKETS - max_exact)).astype(jnp.int32)
    large = jnp.minimum(large, REL_BUCKETS - 1)
    bucket = jnp.where(dist < max_exact, dist, large).astype(jnp.int32)
    return pl.pallas_call(
        _bias_kernel,
        grid=(N_HEADS,),
        in_specs=[pl.BlockSpec(memory_space=pltpu.SMEM),
                  pl.BlockSpec((BLOCK, 2 * BLOCK), lambda h: (0, 0))],
        out_specs=pl.BlockSpec((None, BLOCK, 2 * BLOCK), lambda h: (h, 0, 0)),
        out_shape=jax.ShapeDtypeStruct((N_HEADS, BLOCK, 2 * BLOCK), F32),
        name="band_bias",
    )(rel_bias, bucket)


def _head_copies(group, kv_head):
    low_half = lax.broadcasted_iota(jnp.int32, group.shape, 1) < HEAD_DIM
    if kv_head == 0:
        low = jnp.where(low_half, group, 0.0)
        high = pltpu.roll(low, HEAD_DIM, axis=1)
    else:
        high = jnp.where(low_half, 0.0, group)
        low = pltpu.roll(high, HEAD_DIM, axis=1)
    return low, high


def _group_scores(q_group, k_group, kv_head):
    q_rows = jnp.concatenate([q_group[:, p * V7X_LANES:(p + 1) * V7X_LANES]
                              for p in range(PAIRS)], axis=0)
    contract_lanes = (((1,), (1,)), ((), ()))
    scores = [lax.dot_general(q_rows, k.astype(BF16), contract_lanes, preferred_element_type=F32)
              .reshape(PAIRS, BLOCK, 2 * BLOCK) for k in _head_copies(k_group, kv_head)]
    return jnp.concatenate(scores, axis=0)


def _group_outputs(scores, gate, v_group, kv_head, first_key, sinks, bias):
    qq = lax.broadcasted_iota(jnp.int32, (BLOCK, 2 * BLOCK), 0)
    kk = lax.broadcasted_iota(jnp.int32, (BLOCK, 2 * BLOCK), 1)
    dist = qq + BLOCK - kk
    valid = (dist >= 0) & (dist < WINDOW) & (kk >= first_key)
    s = jnp.where(valid[None], scores + bias, MASK_VALUE)
    sink = sinks[:, :, :1]
    m = jnp.maximum(jnp.max(s, axis=-1, keepdims=True), sink)
    e = jnp.exp(s - m).astype(BF16)
    sink_term = jnp.exp(sink - m)

    lane = lax.broadcasted_iota(jnp.int32, (2 * BLOCK, V7X_LANES), 1)
    ones = [jnp.where(lane < HEAD_DIM, 1.0, 0.0), jnp.where(lane < HEAD_DIM, 0.0, 1.0)]
    rows = PAIRS * BLOCK
    acc = None
    for parity, v in enumerate(_head_copies(v_group, kv_head)):
        v_wide = jnp.concatenate([v, ones[parity]], axis=1).astype(BF16)
        part = jnp.dot(e[parity * PAIRS:(parity + 1) * PAIRS].reshape(rows, 2 * BLOCK), v_wide,
                       preferred_element_type=F32)
        acc = part if acc is None else acc + part
    low_half = lax.broadcasted_iota(jnp.int32, (rows, V7X_LANES), 1) < HEAD_DIM
    denom = acc[:, V7X_LANES:] + jnp.where(low_half, sink_term[:PAIRS].reshape(rows, 1),
                                           sink_term[PAIRS:].reshape(rows, 1))
    attn = acc[:, :V7X_LANES] / denom
    attn = jnp.concatenate([attn[p * BLOCK:(p + 1) * BLOCK] for p in range(PAIRS)], axis=1)
    return (attn * (gate * _sigmoid(gate))).astype(BF16)


def _stage_weight(w_hbm, layer, dst_ref, ring_ref, sems):
    _, rows, cols = w_hbm.shape
    chunks = rows // STAGE_ROWS

    def copy(k, slot):
        return pltpu.make_async_copy(
            w_hbm.at[layer, pl.ds(k * STAGE_ROWS, STAGE_ROWS), :],
            ring_ref.at[slot, :, pl.ds(0, cols)], sems.at[slot])

    for k in range(min(STAGE_SLOTS - 1, chunks)):
        copy(k, k).start()

    def step(k, carry):
        slot = k % STAGE_SLOTS
        copy(k, slot).wait()
        ahead = k + STAGE_SLOTS - 1

        @pl.when(ahead < chunks)
        def _():
            copy(ahead, ahead % STAGE_SLOTS).start()

        r0 = pl.multiple_of(k * STAGE_ROWS, STAGE_ROWS)
        dst_ref[pl.ds(r0, STAGE_ROWS), 0:cols] = ring_ref[slot, :, 0:cols].astype(BF16)
        return carry

    lax.fori_loop(0, chunks, step, 0)


def _layer_kernel(layer, x_ref, p_ref, win_hbm, bin_ref, wout_hbm, wgate_hbm, wple_hbm,
                  wpool_ref, pscale_ref, bias_ref, sinks_ref, gain_ref, lnb_ref, o_ref,
                  kv_ref, u_ref, c_ref, win_ref, wout_ref, wgate_ref, wple_ref, ring_ref, sems):
    j = pl.program_id(1)

    @pl.when((pl.program_id(0) == 0) & (j == 0))
    def _():
        _stage_weight(win_hbm, layer, win_ref, ring_ref, sems)
        _stage_weight(wout_hbm, layer, wout_ref, ring_ref, sems)
        _stage_weight(wgate_hbm, layer, wgate_ref, ring_ref, sems)
        _stage_weight(wple_hbm, layer, wple_ref, ring_ref, sems)

    @pl.when(j == 0)
    def _():
        kv_ref[0:BLOCK, :] = jnp.zeros((BLOCK, 2 * KV_COLS), F32)
        u_ref[0:POOL_TAIL, :] = jnp.zeros((POOL_TAIL, POOL_WIDTH), F32)

    xb = x_ref[...].astype(BF16)
    pb = p_ref[...].astype(BF16)

    def project(lo, hi):
        return (jnp.dot(xb, win_ref[:, lo:hi], preferred_element_type=F32) + bin_ref[:, lo:hi])

    q = (project(0, Q_END) * ATTN_SCALE).astype(BF16)
    kv_ref[BLOCK:, :] = project(Q_END, KV_END)
    ga = project(KV_END, GA_END)

    groups = []
    for blk in range(BLOCKS_PER_TILE):
        rows = slice(blk * BLOCK, (blk + 1) * BLOCK)
        first_key = jnp.where(j == 0, BLOCK, 0) if blk == 0 else 0
        kv = kv_ref[blk * BLOCK:(blk + 2) * BLOCK, :]
        for kv_head in range(N_KV_HEADS):
            cols = slice(kv_head * GROUP_COLS, (kv_head + 1) * GROUP_COLS)
            scores = _group_scores(q[rows, cols], kv[:, :V7X_LANES], kv_head)
            groups.append((rows, cols, kv_head, first_key, kv[:, V7X_LANES:], scores))
    ple_chunks = []
    gb_chunks = []
    for n, (rows, cols, kv_head, first_key, v_group, scores) in enumerate(groups):
        heads = slice(kv_head * KV_GROUP, (kv_head + 1) * KV_GROUP)
        pc = (n * POOL_SLICE) % POOL_WIDTH
        if n * POOL_SLICE < POOL_WIDTH:
            u_ref[POOL_TAIL:, pc:pc + POOL_SLICE] = project(GA_END + pc, GA_END + pc + POOL_SLICE)
        else:
            gb_chunks.append(project(U_END + pc, U_END + pc + POOL_SLICE))
        ec = slice(n * EMB_COLS, (n + 1) * EMB_COLS)
        gate_pre = jnp.dot(xb, wgate_ref[:, ec], preferred_element_type=F32)
        emb = jnp.dot(pb, wple_ref[:, ec], preferred_element_type=F32)
        c_ref[rows, cols] = _group_outputs(scores, ga[rows, cols], v_group, kv_head,
                                           first_key, sinks_ref[heads], bias_ref[heads])
        ple_chunks.append(_sigmoid(gate_pre) * emb)

    gb = jnp.concatenate(gb_chunks, axis=1)
    pos = j * TILE + lax.broadcasted_iota(jnp.int32, (TILE, 1), 0)
    for grp, window in enumerate(POOL_WINDOWS):
        c0 = grp * POOL_GROUP_DIM
        cols = slice(c0, c0 + POOL_GROUP_DIM)
        total = u_ref[:, cols]
        shift = 1
        while shift < window:
            total = total + pltpu.roll(total, shift, axis=0)
            shift *= 2
        total = total[POOL_TAIL:]
        u = u_ref[POOL_TAIL:, cols]
        count = jnp.minimum(pos + 1, window).astype(F32)
        diff = total * (1.0 / count) - u
        mixed = jnp.dot(diff.astype(BF16), wpool_ref[grp], preferred_element_type=F32)
        g = gb[:, cols]
        pooled = mixed * pscale_ref[:, cols] * (g * _sigmoid(g))
        c_ref[:, ATTN_WIDTH + c0:ATTN_WIDTH + c0 + POOL_GROUP_DIM] = pooled.astype(c_ref.dtype)

    kv_ref[0:BLOCK, :] = kv_ref[TILE:TILE + BLOCK, :]
    u_ref[0:POOL_TAIL, :] = u_ref[TILE:TILE + POOL_TAIL, :]

    ple = jnp.concatenate(ple_chunks, axis=1)
    for blk in range(BLOCKS_PER_TILE):
        rows = slice(blk * BLOCK, (blk + 1) * BLOCK)
        mix = jnp.dot(c_ref[rows, :], wout_ref[:, :D_MODEL], preferred_element_type=F32)
        y = DEEPNORM_ALPHA * x_ref[rows, :] + mix + ple[rows]
        mu = jnp.mean(y, axis=-1, keepdims=True)
        yc = y - mu
        var = jnp.mean(yc * yc, axis=-1, keepdims=True)
        o_ref[rows, :] = yc * lax.rsqrt(var + LN_EPS) * gain_ref[...] + lnb_ref[...]


def _layer(x3, p4, sinks, band_bias, w_in, b_in, w_out, w_gate, w_ple, w_pool, pool_scale,
           ln_gain, ln_bias, layer):
    batch, seq, _ = x3.shape
    tile = lambda b, j: (b, j, 0)
    per_layer = lambda b, j: (layer, 0, 0)
    per_layer4 = lambda b, j: (layer, 0, 0, 0)
    in_hbm = pl.BlockSpec(memory_space=pl.ANY)
    return pl.pallas_call(
        functools.partial(_layer_kernel, layer),
        grid=(batch, seq // TILE),
        in_specs=[
            pl.BlockSpec((None, TILE, D_MODEL), tile),
            pl.BlockSpec((None, None, TILE, PLE_DIM), lambda b, j: (layer, b, j, 0)),
            in_hbm,
            _resident((None, 1, IN_COLS), per_layer),
            in_hbm,
            in_hbm,
            in_hbm,
            _resident((None, len(POOL_WINDOWS), POOL_GROUP_DIM, POOL_GROUP_DIM), per_layer4),
            _resident((None, 1, POOL_WIDTH), per_layer),
            _resident((N_HEADS, BLOCK, 2 * BLOCK), lambda b, j: (0, 0, 0)),
            _resident((None, N_HEADS, 1, V7X_LANES), per_layer4),
            _resident((None, 1, D_MODEL), per_layer),
            _resident((None, 1, D_MODEL), per_layer),
        ],
        out_specs=pl.BlockSpec((None, TILE, D_MODEL), tile),
        out_shape=jax.ShapeDtypeStruct((batch, seq, D_MODEL), F32),
        scratch_shapes=[pltpu.VMEM((BLOCK + TILE, 2 * KV_COLS), F32),
                        pltpu.VMEM((POOL_TAIL + TILE, POOL_WIDTH), F32),
                        pltpu.VMEM((TILE, D_MODEL), BF16),
                        pltpu.VMEM((D_MODEL, IN_COLS), BF16),
                        pltpu.VMEM((D_MODEL, PADDED_COLS), BF16),
                        pltpu.VMEM((D_MODEL, PADDED_COLS), BF16),
                        pltpu.VMEM((PLE_DIM, PADDED_COLS), BF16),
                        pltpu.VMEM((STAGE_SLOTS, STAGE_ROWS, IN_COLS), F32),
                        pltpu.SemaphoreType.DMA((STAGE_SLOTS,))],
        compiler_params=pltpu.CompilerParams(
            dimension_semantics=("arbitrary", "arbitrary"),
            vmem_limit_bytes=V7X_VMEM_LIMIT_BYTES),
        name="hybrid_layer",
    )(x3, p4, w_in, b_in, w_out, w_gate, w_ple, w_pool, pool_scale, band_bias, sinks,
      ln_gain, ln_bias)


def kernel(x, p, w_in, b_in, w_out, attn_sinks, rel_bias, w_pool, pool_scale, w_ple, w_gate_ple,
           ln_gain, ln_bias):
    batch, seq, _ = x.shape
    assert x.shape == (batch, seq, D_MODEL) and seq % TILE == 0
    assert w_in.shape == (DEPTH, D_MODEL, IN_COLS)

    w_pool_b = w_pool.astype(BF16)
    b_in3 = b_in[:, None, :]
    pool_scale3 = pool_scale[:, None, :]
    ln_gain3 = ln_gain[:, None, :]
    ln_bias3 = ln_bias[:, None, :]

    order = jnp.array(_group_order(), jnp.int32)
    band_bias = _band_bias(rel_bias)[order]
    sinks = jnp.broadcast_to(attn_sinks[:, order, None, None], (DEPTH, N_HEADS, 1, V7X_LANES))
    for layer in range(DEPTH):
        x = _layer(x, p, sinks, band_bias, w_in, b_in3, w_out, w_gate_ple, w_ple,
                   w_pool_b, pool_scale3, ln_gain3, ln_bias3, layer)
    return x
```

```python
import functools
import math

import jax
import jax.numpy as jnp
from jax import lax
from jax.experimental import pallas as pl
from jax.experimental.pallas import tpu as pltpu

D_MODEL = 2048
DEPTH = 4
PLE_DIM = 256
ATTN_WIDTH = D_MODEL // 2
POOL_WIDTH = D_MODEL - ATTN_WIDTH
HEAD_DIM = 64
N_HEADS = ATTN_WIDTH // HEAD_DIM
N_KV_HEADS = max(1, N_HEADS // 8)
KV_GROUP = N_HEADS // N_KV_HEADS
WINDOW = 128
BLOCK = WINDOW
POOL_WINDOWS = (2, 4, 8, 16)
POOL_GROUP_DIM = POOL_WIDTH // len(POOL_WINDOWS)
POOL_TAIL = 16
REL_BUCKETS = 32
REL_MAX_DIST = 128
LN_EPS = 1e-5
DEEPNORM_ALPHA = (2.0 * DEPTH) ** 0.25
Q_COLS = N_HEADS * HEAD_DIM
KV_COLS = N_KV_HEADS * HEAD_DIM
IN_COLS = Q_COLS + 2 * KV_COLS + ATTN_WIDTH + 2 * POOL_WIDTH
Q_END = Q_COLS
KV_END = Q_END + 2 * KV_COLS
GA_END = KV_END + ATTN_WIDTH
U_END = GA_END + POOL_WIDTH
ATTN_SCALE = 1.0 / math.sqrt(HEAD_DIM)
MASK_VALUE = -1e30

V7X_LANES = 128
PADDED_COLS = D_MODEL + V7X_LANES
V7X_VMEM_LIMIT_BYTES = 62 * 1024 * 1024
STAGE_ROWS = 32
STAGE_SLOTS = 4

TILE = 2 * BLOCK
BLOCKS_PER_TILE = TILE // BLOCK
assert 2 * HEAD_DIM == V7X_LANES
PAIRS = KV_GROUP // 2
GROUP_COLS = KV_GROUP * HEAD_DIM
ATTN_GROUPS = BLOCKS_PER_TILE * N_KV_HEADS
EMB_COLS = D_MODEL // ATTN_GROUPS
POOL_SLICE = 2 * POOL_WIDTH // ATTN_GROUPS
assert POOL_WIDTH % POOL_SLICE == 0
BF16 = jnp.bfloat16
F32 = jnp.float32


def _sigmoid(z):
    return 1.0 / (1.0 + jnp.exp(-z))


def _resident(block_shape, index_map):
    return pl.BlockSpec(block_shape, index_map, pipeline_mode=pl.Buffered(1))


def _group_order():
    return [kv * KV_GROUP + 2 * pair + parity
            for kv in range(N_KV_HEADS) for parity in range(2) for pair in range(PAIRS)]


def _bias_kernel(rel_ref, bucket_ref, o_ref):
    h = pl.program_id(0)
    bucket = bucket_ref[...]
    acc = jnp.zeros(bucket.shape, F32)
    for b in range(REL_BUCKETS):
        acc = jnp.where(bucket == b, rel_ref[b, h], acc)
    o_ref[...] = acc


def _band_bias(rel_bias):
    qq = jnp.arange(BLOCK)[:, None]
    kk = jnp.arange(2 * BLOCK)[None, :]
    dist = jnp.maximum(qq + BLOCK - kk, 0)
    max_exact = REL_BUCKETS // 2
    d_f = jnp.maximum(dist, 1).astype(F32)
    large = max_exact + (jnp.log(d_f / max_exact) / math.log(REL_MAX_DIST / max_exact)
                         * (REL_BUCKETS - max_exact)).astype(jnp.int32)
    large = jnp.minimum(large, REL_BUCKETS - 1)
    bucket = jnp.where(dist < max_exact, dist, large).astype(jnp.int32)
    return pl.pallas_call(
        _bias_kernel,
        grid=(N_HEADS,),
        in_specs=[pl.BlockSpec(memory_space=pltpu.SMEM),
                  pl.BlockSpec((BLOCK, 2 * BLOCK), lambda h: (0, 0))],
        out_specs=pl.BlockSpec((None, BLOCK, 2 * BLOCK), lambda h: (h, 0, 0)),
        out_shape=jax.ShapeDtypeStruct((N_HEADS, BLOCK, 2 * BLOCK), F32),
        name="band_bias",
    )(rel_bias, bucket)


def _head_copies(group, kv_head):
    low_half = lax.broadcasted_iota(jnp.int32, group.shape, 1) < HEAD_DIM
    if kv_head == 0:
        low = jnp.where(low_half, group, 0.0)
        high = pltpu.roll(low, HEAD_DIM, axis=1)
    else:
        high = jnp.where(low_half, 0.0, group)
        low = pltpu.roll(high, HEAD_DIM, axis=1)
    return low, high


def _group_scores(q_group, k_group, kv_head):
    q_rows = jnp.concatenate([q_group[:, p * V7X_LANES:(p + 1) * V7X_LANES]
                              for p in range(PAIRS)], axis=0)
    contract_lanes = (((1,), (1,)), ((), ()))
    scores = [lax.dot_general(q_rows, k.astype(BF16), contract_lanes, preferred_element_type=F32)
              .reshape(PAIRS, BLOCK, 2 * BLOCK) for k in _head_copies(k_group, kv_head)]
    return jnp.concatenate(scores, axis=0)


def _group_outputs(scores, gate, v_group, kv_head, first_key, sinks, bias):
    qq = lax.broadcasted_iota(jnp.int32, (BLOCK, 2 * BLOCK), 0)
    kk = lax.broadcasted_iota(jnp.int32, (BLOCK, 2 * BLOCK), 1)
    dist = qq + BLOCK - kk
    valid = (dist >= 0) & (dist < WINDOW) & (kk >= first_key)
    s = jnp.where(valid[None], scores + bias, MASK_VALUE)
    sink = sinks[:, :, :1]
    m = jnp.maximum(jnp.max(s, axis=-1, keepdims=True), sink)
    e = jnp.exp(s - m).astype(BF16)
    sink_term = jnp.exp(sink - m)

    lane = lax.broadcasted_iota(jnp.int32, (2 * BLOCK, V7X_LANES), 1)
    ones = [jnp.where(lane < HEAD_DIM, 1.0, 0.0), jnp.where(lane < HEAD_DIM, 0.0, 1.0)]
    rows = PAIRS * BLOCK
    acc = None
    for parity, v in enumerate(_head_copies(v_group, kv_head)):
        v_wide = jnp.concatenate([v, ones[parity]], axis=1).astype(BF16)
        part = jnp.dot(e[parity * PAIRS:(parity + 1) * PAIRS].reshape(rows, 2 * BLOCK), v_wide,
                       preferred_element_type=F32)
        acc = part if acc is None else acc + part
    low_half = lax.broadcasted_iota(jnp.int32, (rows, V7X_LANES), 1) < HEAD_DIM
    denom = acc[:, V7X_LANES:] + jnp.where(low_half, sink_term[:PAIRS].reshape(rows, 1),
                                           sink_term[PAIRS:].reshape(rows, 1))
    attn = acc[:, :V7X_LANES] / denom
    attn = jnp.concatenate([attn[p * BLOCK:(p + 1) * BLOCK] for p in range(PAIRS)], axis=1)
    return (attn * (gate * _sigmoid(gate))).astype(BF16)


def _stage_weight(w_hbm, layer, dst_ref, ring_ref, sems):
    _, rows, cols = w_hbm.shape
    chunks = rows // STAGE_ROWS

    assert chunks % STAGE_SLOTS == 0

    def copy(k, slot):
        return pltpu.make_async_copy(
            w_hbm.at[layer, pl.ds(k * STAGE_ROWS, STAGE_ROWS), :],
            ring_ref.at[slot, :, pl.ds(0, cols)], sems.at[slot])

    for slot in range(STAGE_SLOTS):
        copy(slot, slot).start(priority=slot % 2)

    def ring_turn(turn, carry):
        for slot in range(STAGE_SLOTS):
            k = turn * STAGE_SLOTS + slot
            copy(k, slot).wait()
            r0 = pl.multiple_of(k * STAGE_ROWS, STAGE_ROWS)
            dst_ref[pl.ds(r0, STAGE_ROWS), 0:cols] = ring_ref[slot, :, 0:cols].astype(BF16)

            @pl.when(k + STAGE_SLOTS < chunks)
            def _():
                copy(k + STAGE_SLOTS, slot).start(priority=slot % 2)
        return carry

    lax.fori_loop(0, chunks // STAGE_SLOTS, ring_turn, 0)


def _layer_kernel(layer, x_ref, p_ref, win_hbm, bin_ref, wout_hbm, wgate_hbm, wple_hbm,
                  wpool_ref, pscale_ref, bias_ref, sinks_ref, gain_ref, lnb_ref, o_ref,
                  kv_ref, u_ref, c_ref, win_ref, wout_ref, wgate_ref, wple_ref, ring_ref, sems):
    j = pl.program_id(1)

    @pl.when((pl.program_id(0) == 0) & (j == 0))
    def _():
        _stage_weight(win_hbm, layer, win_ref, ring_ref, sems)
        _stage_weight(wout_hbm, layer, wout_ref, ring_ref, sems)
        _stage_weight(wgate_hbm, layer, wgate_ref, ring_ref, sems)
        _stage_weight(wple_hbm, layer, wple_ref, ring_ref, sems)

    @pl.when(j == 0)
    def _():
        kv_ref[0:BLOCK, :] = jnp.zeros((BLOCK, 2 * KV_COLS), F32)
        u_ref[0:POOL_TAIL, :] = jnp.zeros((POOL_TAIL, POOL_WIDTH), F32)

    xb = x_ref[...].astype(BF16)
    pb = p_ref[...].astype(BF16)

    def project(lo, hi):
        return (jnp.dot(xb, win_ref[:, lo:hi], preferred_element_type=F32) + bin_ref[:, lo:hi])

    q = (project(0, Q_END) * ATTN_SCALE).astype(BF16)
    kv_ref[BLOCK:, :] = project(Q_END, KV_END)
    ga = project(KV_END, GA_END)

    groups = []
    for blk in range(BLOCKS_PER_TILE):
        rows = slice(blk * BLOCK, (blk + 1) * BLOCK)
        first_key = jnp.where(j == 0, BLOCK, 0) if blk == 0 else 0
        kv = kv_ref[blk * BLOCK:(blk + 2) * BLOCK, :]
        for kv_head in range(N_KV_HEADS):
            cols = slice(kv_head * GROUP_COLS, (kv_head + 1) * GROUP_COLS)
            scores = _group_scores(q[rows, cols], kv[:, :V7X_LANES], kv_head)
            groups.append((rows, cols, kv_head, first_key, kv[:, V7X_LANES:], scores))
    ple_chunks = []
    gb_chunks = []
    for n, (rows, cols, kv_head, first_key, v_group, scores) in enumerate(groups):
        heads = slice(kv_head * KV_GROUP, (kv_head + 1) * KV_GROUP)
        pc = (n * POOL_SLICE) % POOL_WIDTH
        if n * POOL_SLICE < POOL_WIDTH:
            u_ref[POOL_TAIL:, pc:pc + POOL_SLICE] = project(GA_END + pc, GA_END + pc + POOL_SLICE)
        else:
            gb_chunks.append(project(U_END + pc, U_END + pc + POOL_SLICE))
        ec = slice(n * EMB_COLS, (n + 1) * EMB_COLS)
        gate_pre = jnp.dot(xb, wgate_ref[:, ec], preferred_element_type=F32)
        emb = jnp.dot(pb, wple_ref[:, ec], preferred_element_type=F32)
        c_ref[rows, cols] = _group_outputs(scores, ga[rows, cols], v_group, kv_head,
                                           first_key, sinks_ref[heads], bias_ref[heads])
        ple_chunks.append(_sigmoid(gate_pre) * emb)

    gb = jnp.concatenate(gb_chunks, axis=1)
    pos = j * TILE + lax.broadcasted_iota(jnp.int32, (TILE, 1), 0)
    for grp, window in enumerate(POOL_WINDOWS):
        c0 = grp * POOL_GROUP_DIM
        cols = slice(c0, c0 + POOL_GROUP_DIM)
        total = u_ref[:, cols]
        shift = 1
        while shift < window:
            total = total + pltpu.roll(total, shift, axis=0)
            shift *= 2
        total = total[POOL_TAIL:]
        u = u_ref[POOL_TAIL:, cols]
        count = jnp.minimum(pos + 1, window).astype(F32)
        diff = total * (1.0 / count) - u
        mixed = jnp.dot(diff.astype(BF16), wpool_ref[grp], preferred_element_type=F32)
        g = gb[:, cols]
        pooled = mixed * pscale_ref[:, cols] * (g * _sigmoid(g))
        c_ref[:, ATTN_WIDTH + c0:ATTN_WIDTH + c0 + POOL_GROUP_DIM] = pooled.astype(c_ref.dtype)

    kv_ref[0:BLOCK, :] = kv_ref[TILE:TILE + BLOCK, :]
    u_ref[0:POOL_TAIL, :] = u_ref[TILE:TILE + POOL_TAIL, :]

    ple = jnp.concatenate(ple_chunks, axis=1)
    for blk in range(BLOCKS_PER_TILE):
        rows = slice(blk * BLOCK, (blk + 1) * BLOCK)
        mix = jnp.dot(c_ref[rows, :], wout_ref[:, :D_MODEL], preferred_element_type=F32)
        y = DEEPNORM_ALPHA * x_ref[rows, :] + mix + ple[rows]
        mu = jnp.mean(y, axis=-1, keepdims=True)
        yc = y - mu
        var = jnp.mean(yc * yc, axis=-1, keepdims=True)
        o_ref[rows, :] = yc * lax.rsqrt(var + LN_EPS) * gain_ref[...] + lnb_ref[...]


def _layer(x3, p4, sinks, band_bias, w_in, b_in, w_out, w_gate, w_ple, w_pool, pool_scale,
           ln_gain, ln_bias, layer):
    batch, seq, _ = x3.shape
    tile = lambda b, j: (b, j, 0)
    per_layer = lambda b, j: (layer, 0, 0)
    per_layer4 = lambda b, j: (layer, 0, 0, 0)
    in_hbm = pl.BlockSpec(memory_space=pl.ANY)
    return pl.pallas_call(
        functools.partial(_layer_kernel, layer),
        grid=(batch, seq // TILE),
        in_specs=[
            pl.BlockSpec((None, TILE, D_MODEL), tile),
            pl.BlockSpec((None, None, TILE, PLE_DIM), lambda b, j: (layer, b, j, 0)),
            in_hbm,
            _resident((None, 1, IN_COLS), per_layer),
            in_hbm,
            in_hbm,
            in_hbm,
            _resident((None, len(POOL_WINDOWS), POOL_GROUP_DIM, POOL_GROUP_DIM), per_layer4),
            _resident((None, 1, POOL_WIDTH), per_layer),
            _resident((N_HEADS, BLOCK, 2 * BLOCK), lambda b, j: (0, 0, 0)),
            _resident((None, N_HEADS, 1, V7X_LANES), per_layer4),
            _resident((None, 1, D_MODEL), per_layer),
            _resident((None, 1, D_MODEL), per_layer),
        ],
        out_specs=pl.BlockSpec((None, TILE, D_MODEL), tile),
        out_shape=jax.ShapeDtypeStruct((batch, seq, D_MODEL), F32),
        scratch_shapes=[pltpu.VMEM((BLOCK + TILE, 2 * KV_COLS), F32),
                        pltpu.VMEM((POOL_TAIL + TILE, POOL_WIDTH), F32),
                        pltpu.VMEM((TILE, D_MODEL), BF16),
                        pltpu.VMEM((D_MODEL, IN_COLS), BF16),
                        pltpu.VMEM((D_MODEL, PADDED_COLS), BF16),
                        pltpu.VMEM((D_MODEL, PADDED_COLS), BF16),
                        pltpu.VMEM((PLE_DIM, PADDED_COLS), BF16),
                        pltpu.VMEM((STAGE_SLOTS, STAGE_ROWS, IN_COLS), F32),
                        pltpu.SemaphoreType.DMA((STAGE_SLOTS,))],
        compiler_params=pltpu.CompilerParams(
            dimension_semantics=("arbitrary", "arbitrary"),
            vmem_limit_bytes=V7X_VMEM_LIMIT_BYTES),
        name="hybrid_layer",
    )(x3, p4, w_in, b_in, w_out, w_gate, w_ple, w_pool, pool_scale, band_bias, sinks,
      ln_gain, ln_bias)


def kernel(x, p, w_in, b_in, w_out, attn_sinks, rel_bias, w_pool, pool_scale, w_ple, w_gate_ple,
           ln_gain, ln_bias):
    batch, seq, _ = x.shape
    assert x.shape == (batch, seq, D_MODEL) and seq % TILE == 0
    assert w_in.shape == (DEPTH, D_MODEL, IN_COLS)

    w_pool_b = w_pool.astype(BF16)
    b_in3 = b_in[:, None, :]
    pool_scale3 = pool_scale[:, None, :]
    ln_gain3 = ln_gain[:, None, :]
    ln_bias3 = ln_bias[:, None, :]

    order = jnp.array(_group_order(), jnp.int32)
    band_bias = _band_bias(rel_bias)[order]
    sinks = jnp.broadcast_to(attn_sinks[:, order, None, None], (DEPTH, N_HEADS, 1, V7X_LANES))
    for layer in range(DEPTH):
        x = _layer(x, p, sinks, band_bias, w_in, b_in3, w_out, w_gate_ple, w_ple,
                   w_pool_b, pool_scale3, ln_gain3, ln_bias3, layer)
    return x
```

```python
import math

import jax
import jax.numpy as jnp
from jax import lax
from jax.experimental import pallas as pl
from jax.experimental.pallas import tpu as pltpu

D_MODEL = 2048
DEPTH = 4
PLE_DIM = 256
ATTN_WIDTH = D_MODEL // 2
POOL_WIDTH = D_MODEL - ATTN_WIDTH
HEAD_DIM = 64
N_HEADS = ATTN_WIDTH // HEAD_DIM
N_KV_HEADS = max(1, N_HEADS // 8)
KV_GROUP = N_HEADS // N_KV_HEADS
WINDOW = 128
BLOCK = WINDOW
POOL_WINDOWS = (2, 4, 8, 16)
POOL_GROUP_DIM = POOL_WIDTH // len(POOL_WINDOWS)
POOL_TAIL = 16
REL_BUCKETS = 32
REL_MAX_DIST = 128
LN_EPS = 1e-5
DEEPNORM_ALPHA = (2.0 * DEPTH) ** 0.25
Q_COLS = N_HEADS * HEAD_DIM
KV_COLS = N_KV_HEADS * HEAD_DIM
IN_COLS = Q_COLS + 2 * KV_COLS + ATTN_WIDTH + 2 * POOL_WIDTH
Q_END = Q_COLS
KV_END = Q_END + 2 * KV_COLS
GA_END = KV_END + ATTN_WIDTH
U_END = GA_END + POOL_WIDTH
ATTN_SCALE = 1.0 / math.sqrt(HEAD_DIM)
MASK_VALUE = -1e30

V7X_LANES = 128
PADDED_COLS = D_MODEL + V7X_LANES
V7X_VMEM_LIMIT_BYTES = 60 * 1024 * 1024

TILE = 2 * BLOCK
BLOCKS_PER_TILE = TILE // BLOCK
assert 2 * HEAD_DIM == V7X_LANES
PAIRS = KV_GROUP // 2
GROUP_COLS = KV_GROUP * HEAD_DIM
ATTN_GROUPS = BLOCKS_PER_TILE * N_KV_HEADS
EMB_COLS = D_MODEL // ATTN_GROUPS
POOL_SLICE = 2 * POOL_WIDTH // ATTN_GROUPS
assert POOL_WIDTH % POOL_SLICE == 0
PREP_ROWS = 256
BF16 = jnp.bfloat16
F32 = jnp.float32


def _sigmoid(z):
    return 1.0 / (1.0 + jnp.exp(-z))


def _layer_norm(y, gain, bias):
    mu = jnp.mean(y, axis=-1, keepdims=True)
    yc = y - mu
    var = jnp.mean(yc * yc, axis=-1, keepdims=True)
    return yc * lax.rsqrt(var + LN_EPS) * gain + bias


def _resident(block_shape, index_map):
    return pl.BlockSpec(block_shape, index_map, pipeline_mode=pl.Buffered(1))


def _group_order():
    return [kv * KV_GROUP + 2 * pair + parity
            for kv in range(N_KV_HEADS) for parity in range(2) for pair in range(PAIRS)]


def _bias_kernel(rel_ref, bucket_ref, o_ref):
    h = pl.program_id(0)
    bucket = bucket_ref[...]
    acc = jnp.zeros(bucket.shape, F32)
    for b in range(REL_BUCKETS):
        acc = jnp.where(bucket == b, rel_ref[b, h], acc)
    o_ref[...] = acc


def _band_bias(rel_bias):
    qq = jnp.arange(BLOCK)[:, None]
    kk = jnp.arange(2 * BLOCK)[None, :]
    dist = jnp.maximum(qq + BLOCK - kk, 0)
    max_exact = REL_BUCKETS // 2
    d_f = jnp.maximum(dist, 1).astype(F32)
    large = max_exact + (jnp.log(d_f / max_exact) / math.log(REL_MAX_DIST / max_exact)
                         * (REL_BUCKETS - max_exact)).astype(jnp.int32)
    large = jnp.minimum(large, REL_BUCKETS - 1)
    bucket = jnp.where(dist < max_exact, dist, large).astype(jnp.int32)
    return pl.pallas_call(
        _bias_kernel,
        grid=(N_HEADS,),
        in_specs=[pl.BlockSpec(memory_space=pltpu.SMEM),
                  pl.BlockSpec((BLOCK, 2 * BLOCK), lambda h: (0, 0))],
        out_specs=pl.BlockSpec((None, BLOCK, 2 * BLOCK), lambda h: (h, 0, 0)),
        out_shape=jax.ShapeDtypeStruct((N_HEADS, BLOCK, 2 * BLOCK), F32),
        name="band_bias",
    )(rel_bias, bucket)


def _to_bf16_kernel(w_ref, o_ref):
    o_ref[:, :D_MODEL] = w_ref[...].astype(BF16)
    o_ref[:, D_MODEL:] = jnp.zeros((o_ref.shape[0], PADDED_COLS - D_MODEL), BF16)


def _to_padded_bf16(w):
    depth, rows, cols = w.shape
    assert cols == D_MODEL and rows % PREP_ROWS == 0
    return pl.pallas_call(
        _to_bf16_kernel,
        grid=(depth, rows // PREP_ROWS),
        in_specs=[pl.BlockSpec((None, PREP_ROWS, D_MODEL), lambda l, r: (l, r, 0))],
        out_specs=pl.BlockSpec((None, PREP_ROWS, PADDED_COLS), lambda l, r: (l, r, 0)),
        out_shape=jax.ShapeDtypeStruct((depth, rows, PADDED_COLS), BF16),
        name="to_padded_bf16",
    )(w)


def _head_copies(group, kv_head):
    low_half = lax.broadcasted_iota(jnp.int32, group.shape, 1) < HEAD_DIM
    if kv_head == 0:
        low = jnp.where(low_half, group, 0.0)
        high = pltpu.roll(low, HEAD_DIM, axis=1)
    else:
        high = jnp.where(low_half, 0.0, group)
        low = pltpu.roll(high, HEAD_DIM, axis=1)
    return low, high


def _group_scores(q_group, k_group, kv_head):
    q_rows = jnp.concatenate([q_group[:, p * V7X_LANES:(p + 1) * V7X_LANES]
                              for p in range(PAIRS)], axis=0)
    contract_lanes = (((1,), (1,)), ((), ()))
    scores = [lax.dot_general(q_rows, k.astype(BF16), contract_lanes, preferred_element_type=F32)
              .reshape(PAIRS, BLOCK, 2 * BLOCK) for k in _head_copies(k_group, kv_head)]
    return jnp.concatenate(scores, axis=0)


def _group_outputs(scores, gate, v_group, kv_head, first_key, sinks, bias):
    qq = lax.broadcasted_iota(jnp.int32, (BLOCK, 2 * BLOCK), 0)
    kk = lax.broadcasted_iota(jnp.int32, (BLOCK, 2 * BLOCK), 1)
    dist = qq + BLOCK - kk
    valid = (dist >= 0) & (dist < WINDOW) & (kk >= first_key)
    s = jnp.where(valid[None], scores + bias, MASK_VALUE)
    sink = sinks[:, :, :1]
    m = jnp.maximum(jnp.max(s, axis=-1, keepdims=True), sink)
    e = jnp.exp(s - m).astype(BF16)
    sink_term = jnp.exp(sink - m)

    lane = lax.broadcasted_iota(jnp.int32, (2 * BLOCK, V7X_LANES), 1)
    ones = [jnp.where(lane < HEAD_DIM, 1.0, 0.0), jnp.where(lane < HEAD_DIM, 0.0, 1.0)]
    rows = PAIRS * BLOCK
    acc = None
    for parity, v in enumerate(_head_copies(v_group, kv_head)):
        v_wide = jnp.concatenate([v, ones[parity]], axis=1).astype(BF16)
        part = jnp.dot(e[parity * PAIRS:(parity + 1) * PAIRS].reshape(rows, 2 * BLOCK), v_wide,
                       preferred_element_type=F32)
        acc = part if acc is None else acc + part
    low_half = lax.broadcasted_iota(jnp.int32, (rows, V7X_LANES), 1) < HEAD_DIM
    denom = acc[:, V7X_LANES:] + jnp.where(low_half, sink_term[:PAIRS].reshape(rows, 1),
                                           sink_term[PAIRS:].reshape(rows, 1))
    attn = acc[:, :V7X_LANES] / denom
    attn = jnp.concatenate([attn[p * BLOCK:(p + 1) * BLOCK] for p in range(PAIRS)], axis=1)
    return (attn * (gate * _sigmoid(gate))).astype(BF16)


def _layer_kernel(x_ref, p_ref, win_ref, bin_ref, wout_ref, wgate_ref, wple_ref,
                  wpool_ref, pscale_ref, bias_ref, sinks_ref, gain_ref, lnb_ref, o_ref,
                  kv_ref, u_ref, c_ref):
    j = pl.program_id(1)

    @pl.when(j == 0)
    def _():
        kv_ref[0:BLOCK, :] = jnp.zeros((BLOCK, 2 * KV_COLS), F32)
        u_ref[0:POOL_TAIL, :] = jnp.zeros((POOL_TAIL, POOL_WIDTH), F32)

    xb = x_ref[...].astype(BF16)
    pb = p_ref[...].astype(BF16)

    def project(lo, hi):
        return (jnp.dot(xb, win_ref[:, lo:hi], preferred_element_type=F32) + bin_ref[:, lo:hi])

    q = (project(0, Q_END) * ATTN_SCALE).astype(BF16)
    kv_ref[BLOCK:, :] = project(Q_END, KV_END)
    ga = project(KV_END, GA_END)

    groups = []
    for blk in range(BLOCKS_PER_TILE):
        rows = slice(blk * BLOCK, (blk + 1) * BLOCK)
        first_key = jnp.where(j == 0, BLOCK, 0) if blk == 0 else 0
        kv = kv_ref[blk * BLOCK:(blk + 2) * BLOCK, :]
        for kv_head in range(N_KV_HEADS):
            cols = slice(kv_head * GROUP_COLS, (kv_head + 1) * GROUP_COLS)
            scores = _group_scores(q[rows, cols], kv[:, :V7X_LANES], kv_head)
            groups.append((rows, cols, kv_head, first_key, kv[:, V7X_LANES:], scores))
    ple_chunks = []
    gb_chunks = []
    for n, (rows, cols, kv_head, first_key, v_group, scores) in enumerate(groups):
        heads = slice(kv_head * KV_GROUP, (kv_head + 1) * KV_GROUP)
        pc = (n * POOL_SLICE) % POOL_WIDTH
        if n * POOL_SLICE < POOL_WIDTH:
            u_ref[POOL_TAIL:, pc:pc + POOL_SLICE] = project(GA_END + pc, GA_END + pc + POOL_SLICE)
        else:
            gb_chunks.append(project(U_END + pc, U_END + pc + POOL_SLICE))
        ec = slice(n * EMB_COLS, (n + 1) * EMB_COLS)
        gate_pre = jnp.dot(xb, wgate_ref[:, ec], preferred_element_type=F32)
        emb = jnp.dot(pb, wple_ref[:, ec], preferred_element_type=F32)
        c_ref[rows, cols] = _group_outputs(scores, ga[rows, cols], v_group, kv_head,
                                           first_key, sinks_ref[heads], bias_ref[heads])
        ple_chunks.append(_sigmoid(gate_pre) * emb)

    gb = jnp.concatenate(gb_chunks, axis=1)
    pos = j * TILE + lax.broadcasted_iota(jnp.int32, (TILE, 1), 0)
    for grp, window in enumerate(POOL_WINDOWS):
        c0 = grp * POOL_GROUP_DIM
        cols = slice(c0, c0 + POOL_GROUP_DIM)
        total = u_ref[:, cols]
        shift = 1
        while shift < window:
            total = total + pltpu.roll(total, shift, axis=0)
            shift *= 2
        total = total[POOL_TAIL:]
        u = u_ref[POOL_TAIL:, cols]
        count = jnp.minimum(pos + 1, window).astype(F32)
        diff = total * (1.0 / count) - u
        mixed = jnp.dot(diff.astype(BF16), wpool_ref[grp], preferred_element_type=F32)
        g = gb[:, cols]
        pooled = mixed * pscale_ref[:, cols] * (g * _sigmoid(g))
        c_ref[:, ATTN_WIDTH + c0:ATTN_WIDTH + c0 + POOL_GROUP_DIM] = pooled.astype(c_ref.dtype)

    kv_ref[0:BLOCK, :] = kv_ref[TILE:TILE + BLOCK, :]
    u_ref[0:POOL_TAIL, :] = u_ref[TILE:TILE + POOL_TAIL, :]

    ple = jnp.concatenate(ple_chunks, axis=1)
    for blk in range(BLOCKS_PER_TILE):
        rows = slice(blk * BLOCK, (blk + 1) * BLOCK)
        mix = jnp.dot(c_ref[rows, :], wout_ref[:, :D_MODEL], preferred_element_type=F32)
        y = DEEPNORM_ALPHA * x_ref[rows, :] + mix + ple[rows]
        o_ref[rows, :] = _layer_norm(y, gain_ref[...], lnb_ref[...])


def _layer(x3, p4, sinks, band_bias, w_in, b_in, w_out, w_gate, w_ple, w_pool, pool_scale,
           ln_gain, ln_bias, layer):
    batch, seq, _ = x3.shape
    tile = lambda b, j: (b, j, 0)
    per_layer = lambda b, j: (layer, 0, 0)
    per_layer4 = lambda b, j: (layer, 0, 0, 0)
    return pl.pallas_call(
        _layer_kernel,
        grid=(batch, seq // TILE),
        in_specs=[
            pl.BlockSpec((None, TILE, D_MODEL), tile),
            pl.BlockSpec((None, None, TILE, PLE_DIM), lambda b, j: (layer, b, j, 0)),
            _resident((None, D_MODEL, IN_COLS), per_layer),
            _resident((None, 1, IN_COLS), per_layer),
            _resident((None, D_MODEL, PADDED_COLS), per_layer),
            _resident((None, D_MODEL, PADDED_COLS), per_layer),
            _resident((None, PLE_DIM, PADDED_COLS), per_layer),
            _resident((None, len(POOL_WINDOWS), POOL_GROUP_DIM, POOL_GROUP_DIM), per_layer4),
            _resident((None, 1, POOL_WIDTH), per_layer),
            _resident((N_HEADS, BLOCK, 2 * BLOCK), lambda b, j: (0, 0, 0)),
            _resident((None, N_HEADS, 1, V7X_LANES), per_layer4),
            _resident((None, 1, D_MODEL), per_layer),
            _resident((None, 1, D_MODEL), per_layer),
        ],
        out_specs=pl.BlockSpec((None, TILE, D_MODEL), tile),
        out_shape=jax.ShapeDtypeStruct((batch, seq, D_MODEL), F32),
        scratch_shapes=[pltpu.VMEM((BLOCK + TILE, 2 * KV_COLS), F32),
                        pltpu.VMEM((POOL_TAIL + TILE, POOL_WIDTH), F32),
                        pltpu.VMEM((TILE, D_MODEL), BF16)],
        compiler_params=pltpu.CompilerParams(
            dimension_semantics=("arbitrary", "arbitrary"),
            vmem_limit_bytes=V7X_VMEM_LIMIT_BYTES),
        name="hybrid_layer",
    )(x3, p4, w_in, b_in, w_out, w_gate, w_ple, w_pool, pool_scale, band_bias, sinks,
      ln_gain, ln_bias)


def kernel(x, p, w_in, b_in, w_out, attn_sinks, rel_bias, w_pool, pool_scale, w_ple, w_gate_ple,
           ln_gain, ln_bias):
    batch, seq, _ = x.shape
    assert x.shape == (batch, seq, D_MODEL) and seq % TILE == 0
    assert w_in.shape == (DEPTH, D_MODEL, IN_COLS)

    w_in_b = w_in.astype(BF16)
    w_out_b = _to_padded_bf16(w_out)
    w_gate_b = _to_padded_bf16(w_gate_ple)
    w_ple_b = _to_padded_bf16(w_ple)
    w_pool_b = w_pool.astype(BF16)
    b_in3 = b_in[:, None, :]
    pool_scale3 = pool_scale[:, None, :]
    ln_gain3 = ln_gain[:, None, :]
    ln_bias3 = ln_bias[:, None, :]

    order = jnp.array(_group_order(), jnp.int32)
    band_bias = _band_bias(rel_bias)[order]
    sinks = jnp.broadcast_to(attn_sinks[:, order, None, None], (DEPTH, N_HEADS, 1, V7X_LANES))
    for layer in range(DEPTH):
        x = _layer(x, p, sinks, band_bias, w_in_b, b_in3, w_out_b, w_gate_b, w_ple_b,
                   w_pool_b, pool_scale3, ln_gain3, ln_bias3, layer)
    return x
```

```python
import functools
import math

import jax
import jax.numpy as jnp
from jax import lax
from jax.experimental import pallas as pl
from jax.experimental.pallas import tpu as pltpu

D_MODEL = 2048
DEPTH = 4
PLE_DIM = 256
ATTN_WIDTH = D_MODEL // 2
POOL_WIDTH = D_MODEL - ATTN_WIDTH
HEAD_DIM = 64
N_HEADS = ATTN_WIDTH // HEAD_DIM
N_KV_HEADS = max(1, N_HEADS // 8)
KV_GROUP = N_HEADS // N_KV_HEADS
WINDOW = 128
BLOCK = WINDOW
POOL_WINDOWS = (2, 4, 8, 16)
POOL_GROUP_DIM = POOL_WIDTH // len(POOL_WINDOWS)
POOL_TAIL = 16
REL_BUCKETS = 32
REL_MAX_DIST = 128
LN_EPS = 1e-5
DEEPNORM_ALPHA = (2.0 * DEPTH) ** 0.25
Q_COLS = N_HEADS * HEAD_DIM
KV_COLS = N_KV_HEADS * HEAD_DIM
IN_COLS = Q_COLS + 2 * KV_COLS + ATTN_WIDTH + 2 * POOL_WIDTH
Q_END = Q_COLS
KV_END = Q_END + 2 * KV_COLS
GA_END = KV_END + ATTN_WIDTH
U_END = GA_END + POOL_WIDTH
ATTN_SCALE = 1.0 / math.sqrt(HEAD_DIM)
MASK_VALUE = -1e30

V7X_LANES = 128
PADDED_COLS = D_MODEL + V7X_LANES
V7X_VMEM_LIMIT_BYTES = 62 * 1024 * 1024

TILE = 2 * BLOCK
BLOCKS_PER_TILE = TILE // BLOCK
assert 2 * HEAD_DIM == V7X_LANES
PAIRS = KV_GROUP // 2
GROUP_COLS = KV_GROUP * HEAD_DIM
ATTN_GROUPS = BLOCKS_PER_TILE * N_KV_HEADS
EMB_COLS = D_MODEL // ATTN_GROUPS
POOL_SLICE = 2 * POOL_WIDTH // ATTN_GROUPS
assert POOL_WIDTH % POOL_SLICE == 0
PREP_ROWS = 256
BF16 = jnp.bfloat16
F32 = jnp.float32


def _sigmoid(z):
    return 1.0 / (1.0 + jnp.exp(-z))


def _layer_norm(y, gain, bias):
    mu = jnp.mean(y, axis=-1, keepdims=True)
    yc = y - mu
    var = jnp.mean(yc * yc, axis=-1, keepdims=True)
    return yc * lax.rsqrt(var + LN_EPS) * gain + bias


def _resident(block_shape, index_map):
    return pl.BlockSpec(block_shape, index_map, pipeline_mode=pl.Buffered(1))


def _group_order():
    return [kv * KV_GROUP + 2 * pair + parity
            for kv in range(N_KV_HEADS) for parity in range(2) for pair in range(PAIRS)]


def _bias_kernel(rel_ref, bucket_ref, o_ref):
    h = pl.program_id(0)
    bucket = bucket_ref[...]
    acc = jnp.zeros(bucket.shape, F32)
    for b in range(REL_BUCKETS):
        acc = jnp.where(bucket == b, rel_ref[b, h], acc)
    o_ref[...] = acc


def _band_bias(rel_bias):
    qq = jnp.arange(BLOCK)[:, None]
    kk = jnp.arange(2 * BLOCK)[None, :]
    dist = jnp.maximum(qq + BLOCK - kk, 0)
    max_exact = REL_BUCKETS // 2
    d_f = jnp.maximum(dist, 1).astype(F32)
    large = max_exact + (jnp.log(d_f / max_exact) / math.log(REL_MAX_DIST / max_exact)
                         * (REL_BUCKETS - max_exact)).astype(jnp.int32)
    large = jnp.minimum(large, REL_BUCKETS - 1)
    bucket = jnp.where(dist < max_exact, dist, large).astype(jnp.int32)
    return pl.pallas_call(
        _bias_kernel,
        grid=(N_HEADS,),
        in_specs=[pl.BlockSpec(memory_space=pltpu.SMEM),
                  pl.BlockSpec((BLOCK, 2 * BLOCK), lambda h: (0, 0))],
        out_specs=pl.BlockSpec((None, BLOCK, 2 * BLOCK), lambda h: (h, 0, 0)),
        out_shape=jax.ShapeDtypeStruct((N_HEADS, BLOCK, 2 * BLOCK), F32),
        name="band_bias",
    )(rel_bias, bucket)


def _to_bf16_kernel(w_ref, o_ref):
    o_ref[:, :D_MODEL] = w_ref[...].astype(BF16)
    o_ref[:, D_MODEL:] = jnp.zeros((o_ref.shape[0], PADDED_COLS - D_MODEL), BF16)


def _to_padded_bf16(w):
    depth, rows, cols = w.shape
    assert cols == D_MODEL and rows % PREP_ROWS == 0
    return pl.pallas_call(
        _to_bf16_kernel,
        grid=(depth, rows // PREP_ROWS),
        in_specs=[pl.BlockSpec((None, PREP_ROWS, D_MODEL), lambda l, r: (l, r, 0))],
        out_specs=pl.BlockSpec((None, PREP_ROWS, PADDED_COLS), lambda l, r: (l, r, 0)),
        out_shape=jax.ShapeDtypeStruct((depth, rows, PADDED_COLS), BF16),
        name="to_padded_bf16",
    )(w)


def _head_copies(group, kv_head):
    low_half = lax.broadcasted_iota(jnp.int32, group.shape, 1) < HEAD_DIM
    if kv_head == 0:
        low = jnp.where(low_half, group, 0.0)
        high = pltpu.roll(low, HEAD_DIM, axis=1)
    else:
        high = jnp.where(low_half, 0.0, group)
        low = pltpu.roll(high, HEAD_DIM, axis=1)
    return low, high


def _group_scores(q_group, k_group, kv_head):
    q_rows = jnp.concatenate([q_group[:, p * V7X_LANES:(p + 1) * V7X_LANES]
                              for p in range(PAIRS)], axis=0)
    contract_lanes = (((1,), (1,)), ((), ()))
    scores = [lax.dot_general(q_rows, k.astype(BF16), contract_lanes, preferred_element_type=F32)
              .reshape(PAIRS, BLOCK, 2 * BLOCK) for k in _head_copies(k_group, kv_head)]
    return jnp.concatenate(scores, axis=0)


def _group_outputs(scores, gate, v_group, kv_head, first_key, sinks, bias):
    qq = lax.broadcasted_iota(jnp.int32, (BLOCK, 2 * BLOCK), 0)
    kk = lax.broadcasted_iota(jnp.int32, (BLOCK, 2 * BLOCK), 1)
    dist = qq + BLOCK - kk
    valid = (dist >= 0) & (dist < WINDOW) & (kk >= first_key)
    s = jnp.where(valid[None], scores + bias, MASK_VALUE)
    sink = sinks[:, :, :1]
    m = jnp.maximum(jnp.max(s, axis=-1, keepdims=True), sink)
    e = jnp.exp(s - m).astype(BF16)
    sink_term = jnp.exp(sink - m)

    lane = lax.broadcasted_iota(jnp.int32, (2 * BLOCK, V7X_LANES), 1)
    ones = [jnp.where(lane < HEAD_DIM, 1.0, 0.0), jnp.where(lane < HEAD_DIM, 0.0, 1.0)]
    rows = PAIRS * BLOCK
    acc = None
    for parity, v in enumerate(_head_copies(v_group, kv_head)):
        v_wide = jnp.concatenate([v, ones[parity]], axis=1).astype(BF16)
        part = jnp.dot(e[parity * PAIRS:(parity + 1) * PAIRS].reshape(rows, 2 * BLOCK), v_wide,
                       preferred_element_type=F32)
        acc = part if acc is None else acc + part
    low_half = lax.broadcasted_iota(jnp.int32, (rows, V7X_LANES), 1) < HEAD_DIM
    denom = acc[:, V7X_LANES:] + jnp.where(low_half, sink_term[:PAIRS].reshape(rows, 1),
                                           sink_term[PAIRS:].reshape(rows, 1))
    attn = acc[:, :V7X_LANES] / denom
    attn = jnp.concatenate([attn[p * BLOCK:(p + 1) * BLOCK] for p in range(PAIRS)], axis=1)
    return (attn * (gate * _sigmoid(gate))).astype(BF16)


class _NextWeights:
    def __init__(self, layer, weights, sems, step, steps):
        self.layer, self.weights, self.sems, self.step, self.steps = layer, weights, sems, step, steps

    def _reads(self, slice_index):
        copies = []
        for n, (src, _, stage_in, _) in enumerate(self.weights):
            rows = stage_in.shape[0]
            copies.append(pltpu.make_async_copy(
                src.at[self.layer, pl.ds(slice_index * rows, rows), :], stage_in,
                self.sems.at[0, n]))
        return copies

    def _writes(self, slice_index):
        copies = []
        for n, (_, dst, _, stage_out) in enumerate(self.weights):
            rows = stage_out.shape[0]
            copies.append(pltpu.make_async_copy(
                stage_out, dst.at[pl.ds(slice_index * rows, rows), :], self.sems.at[1, n]))
        return copies

    def begin_step(self):
        step, last = self.step, self.steps - 1

        @pl.when(step == 0)
        def _():
            for _, _, _, stage_out in self.weights:
                stage_out[...] = jnp.zeros(stage_out.shape, stage_out.dtype)
            for copy in self._reads(0):
                copy.start()

        for copy in self._reads(step):
            copy.wait()

        @pl.when(step > 0)
        def _():
            for copy in self._writes(step - 1):
                copy.wait()

        for _, _, stage_in, stage_out in self.weights:
            stage_out[:, 0:stage_in.shape[1]] = stage_in[...].astype(BF16)
        for copy in self._writes(step):
            copy.start()
        for copy in self._reads(jnp.minimum(step + 1, last)):
            copy.start()

    def end_step(self):
        step, last = self.step, self.steps - 1

        @pl.when(step == last)
        def _():
            for copy in self._writes(last) + self._reads(last):
                copy.wait()


def _layer_kernel(convert_next, next_layer, steps, *refs):
    (x_ref, p_ref, win_ref, bin_ref, wout_ref, wgate_ref, wple_ref, wpool_ref, pscale_ref,
     bias_ref, sinks_ref, gain_ref, lnb_ref) = refs[:13]
    if convert_next:
        next_f32 = refs[13:16]
        o_ref = refs[16]
        next_bf16 = refs[17:20]
        kv_ref, u_ref, c_ref = refs[20:23]
        stage_in, stage_out, sems = refs[23:26], refs[26:29], refs[29]
    else:
        o_ref = refs[13]
        kv_ref, u_ref, c_ref = refs[14:17]
    j = pl.program_id(1)

    if convert_next:
        next_weights = _NextWeights(next_layer, list(zip(next_f32, next_bf16, stage_in, stage_out)),
                                    sems, pl.program_id(0) * pl.num_programs(1) + j, steps)
        next_weights.begin_step()

    @pl.when(j == 0)
    def _():
        kv_ref[0:BLOCK, :] = jnp.zeros((BLOCK, 2 * KV_COLS), F32)
        u_ref[0:POOL_TAIL, :] = jnp.zeros((POOL_TAIL, POOL_WIDTH), F32)

    xb = x_ref[...].astype(BF16)
    pb = p_ref[...].astype(BF16)

    def project(lo, hi):
        return (jnp.dot(xb, win_ref[:, lo:hi], preferred_element_type=F32) + bin_ref[:, lo:hi])

    q = (project(0, Q_END) * ATTN_SCALE).astype(BF16)
    kv_ref[BLOCK:, :] = project(Q_END, KV_END)
    ga = project(KV_END, GA_END)

    groups = []
    for blk in range(BLOCKS_PER_TILE):
        rows = slice(blk * BLOCK, (blk + 1) * BLOCK)
        first_key = jnp.where(j == 0, BLOCK, 0) if blk == 0 else 0
        kv = kv_ref[blk * BLOCK:(blk + 2) * BLOCK, :]
        for kv_head in range(N_KV_HEADS):
            cols = slice(kv_head * GROUP_COLS, (kv_head + 1) * GROUP_COLS)
            scores = _group_scores(q[rows, cols], kv[:, :V7X_LANES], kv_head)
            groups.append((rows, cols, kv_head, first_key, kv[:, V7X_LANES:], scores))
    ple_chunks = []
    gb_chunks = []
    for n, (rows, cols, kv_head, first_key, v_group, scores) in enumerate(groups):
        heads = slice(kv_head * KV_GROUP, (kv_head + 1) * KV_GROUP)
        pc = (n * POOL_SLICE) % POOL_WIDTH
        if n * POOL_SLICE < POOL_WIDTH:
            u_ref[POOL_TAIL:, pc:pc + POOL_SLICE] = project(GA_END + pc, GA_END + pc + POOL_SLICE)
        else:
            gb_chunks.append(project(U_END + pc, U_END + pc + POOL_SLICE))
        ec = slice(n * EMB_COLS, (n + 1) * EMB_COLS)
        gate_pre = jnp.dot(xb, wgate_ref[:, ec], preferred_element_type=F32)
        emb = jnp.dot(pb, wple_ref[:, ec], preferred_element_type=F32)
        c_ref[rows, cols] = _group_outputs(scores, ga[rows, cols], v_group, kv_head,
                                           first_key, sinks_ref[heads], bias_ref[heads])
        ple_chunks.append(_sigmoid(gate_pre) * emb)

    gb = jnp.concatenate(gb_chunks, axis=1)
    pos = j * TILE + lax.broadcasted_iota(jnp.int32, (TILE, 1), 0)
    for grp, window in enumerate(POOL_WINDOWS):
        c0 = grp * POOL_GROUP_DIM
        cols = slice(c0, c0 + POOL_GROUP_DIM)
        total = u_ref[:, cols]
        shift = 1
        while shift < window:
            total = total + pltpu.roll(total, shift, axis=0)
            shift *= 2
        total = total[POOL_TAIL:]
        u = u_ref[POOL_TAIL:, cols]
        count = jnp.minimum(pos + 1, window).astype(F32)
        diff = total * (1.0 / count) - u
        mixed = jnp.dot(diff.astype(BF16), wpool_ref[grp], preferred_element_type=F32)
        g = gb[:, cols]
        pooled = mixed * pscale_ref[:, cols] * (g * _sigmoid(g))
        c_ref[:, ATTN_WIDTH + c0:ATTN_WIDTH + c0 + POOL_GROUP_DIM] = pooled.astype(c_ref.dtype)

    kv_ref[0:BLOCK, :] = kv_ref[TILE:TILE + BLOCK, :]
    u_ref[0:POOL_TAIL, :] = u_ref[TILE:TILE + POOL_TAIL, :]

    ple = jnp.concatenate(ple_chunks, axis=1)
    for blk in range(BLOCKS_PER_TILE):
        rows = slice(blk * BLOCK, (blk + 1) * BLOCK)
        mix = jnp.dot(c_ref[rows, :], wout_ref[:, :D_MODEL], preferred_element_type=F32)
        y = DEEPNORM_ALPHA * x_ref[rows, :] + mix + ple[rows]
        o_ref[rows, :] = _layer_norm(y, gain_ref[...], lnb_ref[...])

    if convert_next:
        next_weights.end_step()


def _layer(x3, p4, sinks, band_bias, layer_weights, next_f32, b_in, w_ple, w_pool, pool_scale,
           ln_gain, ln_bias, layer):
    batch, seq, _ = x3.shape
    steps = batch * (seq // TILE)
    convert_next = next_f32 is not None
    tile = lambda b, j: (b, j, 0)
    whole = lambda b, j: (0, 0)
    per_layer = lambda b, j: (layer, 0, 0)
    per_layer4 = lambda b, j: (layer, 0, 0, 0)
    in_hbm = pl.BlockSpec(memory_space=pl.ANY)
    x_shape = jax.ShapeDtypeStruct((batch, seq, D_MODEL), F32)
    x_spec = pl.BlockSpec((None, TILE, D_MODEL), tile)
    scratch = [pltpu.VMEM((BLOCK + TILE, 2 * KV_COLS), F32),
               pltpu.VMEM((POOL_TAIL + TILE, POOL_WIDTH), F32),
               pltpu.VMEM((TILE, D_MODEL), BF16)]
    if convert_next:
        assert all(w.shape[1] % steps == 0 for w in next_f32)
        slices = [(w.shape[1] // steps, w.shape[2]) for w in next_f32]
        out_shape = [x_shape] + [jax.ShapeDtypeStruct(w.shape, BF16) for w in layer_weights]
        out_specs = [x_spec] + [in_hbm] * len(next_f32)
        scratch += [pltpu.VMEM(s, F32) for s in slices]
        scratch += [pltpu.VMEM((rows, w.shape[1]), BF16)
                    for (rows, _), w in zip(slices, layer_weights)]
        scratch += [pltpu.SemaphoreType.DMA((2, len(next_f32)))]
    else:
        out_shape, out_specs = x_shape, x_spec
    outs = pl.pallas_call(
        functools.partial(_layer_kernel, convert_next, layer + 1, steps),
        grid=(batch, seq // TILE),
        in_specs=[
            x_spec,
            pl.BlockSpec((None, None, TILE, PLE_DIM), lambda b, j: (layer, b, j, 0)),
            _resident((D_MODEL, IN_COLS), whole),
            _resident((None, 1, IN_COLS), per_layer),
            _resident((D_MODEL, PADDED_COLS), whole),
            _resident((D_MODEL, PADDED_COLS), whole),
            _resident((None, PLE_DIM, PADDED_COLS), per_layer),
            _resident((None, len(POOL_WINDOWS), POOL_GROUP_DIM, POOL_GROUP_DIM), per_layer4),
            _resident((None, 1, POOL_WIDTH), per_layer),
            _resident((N_HEADS, BLOCK, 2 * BLOCK), lambda b, j: (0, 0, 0)),
            _resident((None, N_HEADS, 1, V7X_LANES), per_layer4),
            _resident((None, 1, D_MODEL), per_layer),
            _resident((None, 1, D_MODEL), per_layer),
        ] + ([in_hbm] * len(next_f32) if convert_next else []),
        out_specs=out_specs,
        out_shape=out_shape,
        scratch_shapes=scratch,
        compiler_params=pltpu.CompilerParams(
            dimension_semantics=("arbitrary", "arbitrary"),
            vmem_limit_bytes=V7X_VMEM_LIMIT_BYTES),
        name="hybrid_layer",
    )(x3, p4, layer_weights[0], b_in, layer_weights[1], layer_weights[2], w_ple, w_pool,
      pool_scale, band_bias, sinks, ln_gain, ln_bias, *(next_f32 if convert_next else ()))
    return (outs[0], tuple(outs[1:])) if convert_next else (outs, None)


def kernel(x, p, w_in, b_in, w_out, attn_sinks, rel_bias, w_pool, pool_scale, w_ple, w_gate_ple,
           ln_gain, ln_bias):
    batch, seq, _ = x.shape
    assert x.shape == (batch, seq, D_MODEL) and seq % TILE == 0
    assert w_in.shape == (DEPTH, D_MODEL, IN_COLS)

    layer_weights = (w_in[0].astype(BF16),
                     _to_padded_bf16(w_out[:1]).reshape(D_MODEL, PADDED_COLS),
                     _to_padded_bf16(w_gate_ple[:1]).reshape(D_MODEL, PADDED_COLS))
    next_f32 = (w_in, w_out, w_gate_ple)
    w_ple_b = _to_padded_bf16(w_ple)
    w_pool_b = w_pool.astype(BF16)
    b_in3 = b_in[:, None, :]
    pool_scale3 = pool_scale[:, None, :]
    ln_gain3 = ln_gain[:, None, :]
    ln_bias3 = ln_bias[:, None, :]

    order = jnp.array(_group_order(), jnp.int32)
    band_bias = _band_bias(rel_bias)[order]
    sinks = jnp.broadcast_to(attn_sinks[:, order, None, None], (DEPTH, N_HEADS, 1, V7X_LANES))
    for layer in range(DEPTH):
        x, layer_weights = _layer(x, p, sinks, band_bias, layer_weights,
                                  next_f32 if layer + 1 < DEPTH else None, b_in3, w_ple_b,
                                  w_pool_b, pool_scale3, ln_gain3, ln_bias3, layer)
    return x
```

```python
import functools
import math

import jax
import jax.numpy as jnp
from jax import lax
from jax.experimental import pallas as pl
from jax.experimental.pallas import tpu as pltpu

D_MODEL = 2048
DEPTH = 4
PLE_DIM = 256
ATTN_WIDTH = D_MODEL // 2
POOL_WIDTH = D_MODEL - ATTN_WIDTH
HEAD_DIM = 64
N_HEADS = ATTN_WIDTH // HEAD_DIM
N_KV_HEADS = max(1, N_HEADS // 8)
KV_GROUP = N_HEADS // N_KV_HEADS
WINDOW = 128
BLOCK = WINDOW
POOL_WINDOWS = (2, 4, 8, 16)
POOL_GROUP_DIM = POOL_WIDTH // len(POOL_WINDOWS)
POOL_TAIL = 16
REL_BUCKETS = 32
REL_MAX_DIST = 128
LN_EPS = 1e-5
DEEPNORM_ALPHA = (2.0 * DEPTH) ** 0.25
Q_COLS = N_HEADS * HEAD_DIM
KV_COLS = N_KV_HEADS * HEAD_DIM
IN_COLS = Q_COLS + 2 * KV_COLS + ATTN_WIDTH + 2 * POOL_WIDTH
Q_END = Q_COLS
KV_END = Q_END + 2 * KV_COLS
GA_END = KV_END + ATTN_WIDTH
U_END = GA_END + POOL_WIDTH
ATTN_SCALE = 1.0 / math.sqrt(HEAD_DIM)
MASK_VALUE = -1e30

V7X_LANES = 128
PADDED_COLS = D_MODEL + V7X_LANES
V7X_VMEM_LIMIT_BYTES = 62 * 1024 * 1024

TILE = 2 * BLOCK
BLOCKS_PER_TILE = TILE // BLOCK
assert 2 * HEAD_DIM == V7X_LANES
PAIRS = KV_GROUP // 2
GROUP_COLS = KV_GROUP * HEAD_DIM
ATTN_GROUPS = BLOCKS_PER_TILE * N_KV_HEADS
EMB_COLS = D_MODEL // ATTN_GROUPS
POOL_SLICE = 2 * POOL_WIDTH // ATTN_GROUPS
assert POOL_WIDTH % POOL_SLICE == 0
PREP_ROWS = 256
BF16 = jnp.bfloat16
F32 = jnp.float32


def _sigmoid(z):
    return 1.0 / (1.0 + jnp.exp(-z))


def _layer_norm(y, gain, bias):
    mu = jnp.mean(y, axis=-1, keepdims=True)
    yc = y - mu
    var = jnp.mean(yc * yc, axis=-1, keepdims=True)
    return yc * lax.rsqrt(var + LN_EPS) * gain + bias


def _resident(block_shape, index_map):
    return pl.BlockSpec(block_shape, index_map, pipeline_mode=pl.Buffered(1))


def _group_order():
    return [kv * KV_GROUP + 2 * pair + parity
            for kv in range(N_KV_HEADS) for parity in range(2) for pair in range(PAIRS)]


def _bias_kernel(rel_ref, bucket_ref, o_ref):
    h = pl.program_id(0)
    bucket = bucket_ref[...]
    acc = jnp.zeros(bucket.shape, F32)
    for b in range(REL_BUCKETS):
        acc = jnp.where(bucket == b, rel_ref[b, h], acc)
    o_ref[...] = acc


def _band_bias(rel_bias):
    qq = jnp.arange(BLOCK)[:, None]
    kk = jnp.arange(2 * BLOCK)[None, :]
    dist = jnp.maximum(qq + BLOCK - kk, 0)
    max_exact = REL_BUCKETS // 2
    d_f = jnp.maximum(dist, 1).astype(F32)
    large = max_exact + (jnp.log(d_f / max_exact) / math.log(REL_MAX_DIST / max_exact)
                         * (REL_BUCKETS - max_exact)).astype(jnp.int32)
    large = jnp.minimum(large, REL_BUCKETS - 1)
    bucket = jnp.where(dist < max_exact, dist, large).astype(jnp.int32)
    return pl.pallas_call(
        _bias_kernel,
        grid=(N_HEADS,),
        in_specs=[pl.BlockSpec(memory_space=pltpu.SMEM),
                  pl.BlockSpec((BLOCK, 2 * BLOCK), lambda h: (0, 0))],
        out_specs=pl.BlockSpec((None, BLOCK, 2 * BLOCK), lambda h: (h, 0, 0)),
        out_shape=jax.ShapeDtypeStruct((N_HEADS, BLOCK, 2 * BLOCK), F32),
        name="band_bias",
    )(rel_bias, bucket)


def _to_bf16_kernel(w_ref, o_ref):
    o_ref[:, :D_MODEL] = w_ref[...].astype(BF16)
    o_ref[:, D_MODEL:] = jnp.zeros((o_ref.shape[0], PADDED_COLS - D_MODEL), BF16)


def _to_padded_bf16(w, depth):
    _, rows, cols = w.shape
    assert cols == D_MODEL and rows % PREP_ROWS == 0
    return pl.pallas_call(
        _to_bf16_kernel,
        grid=(depth, rows // PREP_ROWS),
        in_specs=[pl.BlockSpec((None, PREP_ROWS, D_MODEL), lambda l, r: (l, r, 0))],
        out_specs=pl.BlockSpec((None, PREP_ROWS, PADDED_COLS), lambda l, r: (l, r, 0)),
        out_shape=jax.ShapeDtypeStruct((depth, rows, PADDED_COLS), BF16),
        name="to_padded_bf16",
    )(w)


def _head_copies(group, kv_head):
    low_half = lax.broadcasted_iota(jnp.int32, group.shape, 1) < HEAD_DIM
    if kv_head == 0:
        low = jnp.where(low_half, group, 0.0)
        high = pltpu.roll(low, HEAD_DIM, axis=1)
    else:
        high = jnp.where(low_half, 0.0, group)
        low = pltpu.roll(high, HEAD_DIM, axis=1)
    return low, high


def _group_scores(q_group, k_group, kv_head):
    q_rows = jnp.concatenate([q_group[:, p * V7X_LANES:(p + 1) * V7X_LANES]
                              for p in range(PAIRS)], axis=0)
    contract_lanes = (((1,), (1,)), ((), ()))
    scores = [lax.dot_general(q_rows, k.astype(BF16), contract_lanes, preferred_element_type=F32)
              .reshape(PAIRS, BLOCK, 2 * BLOCK) for k in _head_copies(k_group, kv_head)]
    return jnp.concatenate(scores, axis=0)


def _group_outputs(scores, gate, v_group, kv_head, first_key, sinks, bias):
    qq = lax.broadcasted_iota(jnp.int32, (BLOCK, 2 * BLOCK), 0)
    kk = lax.broadcasted_iota(jnp.int32, (BLOCK, 2 * BLOCK), 1)
    dist = qq + BLOCK - kk
    valid = (dist >= 0) & (dist < WINDOW) & (kk >= first_key)
    s = jnp.where(valid[None], scores + bias, MASK_VALUE)
    sink = sinks[:, :, :1]
    m = jnp.maximum(jnp.max(s, axis=-1, keepdims=True), sink)
    e = jnp.exp(s - m).astype(BF16)
    sink_term = jnp.exp(sink - m)

    lane = lax.broadcasted_iota(jnp.int32, (2 * BLOCK, V7X_LANES), 1)
    ones = [jnp.where(lane < HEAD_DIM, 1.0, 0.0), jnp.where(lane < HEAD_DIM, 0.0, 1.0)]
    rows = PAIRS * BLOCK
    acc = None
    for parity, v in enumerate(_head_copies(v_group, kv_head)):
        v_wide = jnp.concatenate([v, ones[parity]], axis=1).astype(BF16)
        part = jnp.dot(e[parity * PAIRS:(parity + 1) * PAIRS].reshape(rows, 2 * BLOCK), v_wide,
                       preferred_element_type=F32)
        acc = part if acc is None else acc + part
    low_half = lax.broadcasted_iota(jnp.int32, (rows, V7X_LANES), 1) < HEAD_DIM
    denom = acc[:, V7X_LANES:] + jnp.where(low_half, sink_term[:PAIRS].reshape(rows, 1),
                                           sink_term[PAIRS:].reshape(rows, 1))
    attn = acc[:, :V7X_LANES] / denom
    attn = jnp.concatenate([attn[p * BLOCK:(p + 1) * BLOCK] for p in range(PAIRS)], axis=1)
    return (attn * (gate * _sigmoid(gate))).astype(BF16)


class _NextWeights:
    def __init__(self, layer, weights, sems, step, steps):
        self.layer, self.weights, self.sems, self.step, self.steps = layer, weights, sems, step, steps

    def _reads(self, slice_index):
        copies = []
        for n, (src, _, stage_in, _) in enumerate(self.weights):
            rows = stage_in.shape[0]
            copies.append(pltpu.make_async_copy(
                src.at[self.layer, pl.ds(slice_index * rows, rows), :], stage_in,
                self.sems.at[0, n]))
        return copies

    def _writes(self, slice_index):
        copies = []
        for n, (_, dst, _, stage_out) in enumerate(self.weights):
            rows = stage_out.shape[0]
            copies.append(pltpu.make_async_copy(
                stage_out, dst.at[pl.ds(slice_index * rows, rows), :], self.sems.at[1, n]))
        return copies

    def begin_step(self):
        step, last = self.step, self.steps - 1

        @pl.when(step == 0)
        def _():
            for _, _, _, stage_out in self.weights:
                stage_out[...] = jnp.zeros(stage_out.shape, stage_out.dtype)
            for copy in self._reads(0):
                copy.start()

        for copy in self._reads(step):
            copy.wait()

        @pl.when(step > 0)
        def _():
            for copy in self._writes(step - 1):
                copy.wait()

        for _, _, stage_in, stage_out in self.weights:
            stage_out[:, 0:stage_in.shape[1]] = stage_in[...].astype(BF16)
        for copy in self._writes(step):
            copy.start()
        for copy in self._reads(jnp.minimum(step + 1, last)):
            copy.start()

    def end_step(self):
        step, last = self.step, self.steps - 1

        @pl.when(step == last)
        def _():
            for copy in self._writes(last) + self._reads(last):
                copy.wait()


def _layer_kernel(convert_next, next_layer, steps, *refs):
    (x_ref, p_ref, win_ref, bin_ref, wout_ref, wgate_ref, wple_ref, wpool_ref, pscale_ref,
     bias_ref, sinks_ref, gain_ref, lnb_ref) = refs[:13]
    if convert_next:
        next_f32 = refs[13:16]
        o_ref = refs[16]
        next_bf16 = refs[17:20]
        kv_ref, u_ref, c_ref = refs[20:23]
        stage_in, stage_out, sems = refs[23:26], refs[26:29], refs[29]
    else:
        o_ref = refs[13]
        kv_ref, u_ref, c_ref = refs[14:17]
    j = pl.program_id(1)

    if convert_next:
        next_weights = _NextWeights(next_layer, list(zip(next_f32, next_bf16, stage_in, stage_out)),
                                    sems, pl.program_id(0) * pl.num_programs(1) + j, steps)
        next_weights.begin_step()

    @pl.when(j == 0)
    def _():
        kv_ref[0:BLOCK, :] = jnp.zeros((BLOCK, 2 * KV_COLS), F32)
        u_ref[0:POOL_TAIL, :] = jnp.zeros((POOL_TAIL, POOL_WIDTH), F32)

    xb = x_ref[...].astype(BF16)
    pb = p_ref[...].astype(BF16)

    def project(lo, hi):
        return (jnp.dot(xb, win_ref[:, lo:hi], preferred_element_type=F32) + bin_ref[:, lo:hi])

    q = (project(0, Q_END) * ATTN_SCALE).astype(BF16)
    kv_ref[BLOCK:, :] = project(Q_END, KV_END)
    ga = project(KV_END, GA_END)

    groups = []
    for blk in range(BLOCKS_PER_TILE):
        rows = slice(blk * BLOCK, (blk + 1) * BLOCK)
        first_key = jnp.where(j == 0, BLOCK, 0) if blk == 0 else 0
        kv = kv_ref[blk * BLOCK:(blk + 2) * BLOCK, :]
        for kv_head in range(N_KV_HEADS):
            cols = slice(kv_head * GROUP_COLS, (kv_head + 1) * GROUP_COLS)
            scores = _group_scores(q[rows, cols], kv[:, :V7X_LANES], kv_head)
            groups.append((rows, cols, kv_head, first_key, kv[:, V7X_LANES:], scores))
    ple_chunks = []
    gb_chunks = []
    for n, (rows, cols, kv_head, first_key, v_group, scores) in enumerate(groups):
        heads = slice(kv_head * KV_GROUP, (kv_head + 1) * KV_GROUP)
        pc = (n * POOL_SLICE) % POOL_WIDTH
        if n * POOL_SLICE < POOL_WIDTH:
            u_ref[POOL_TAIL:, pc:pc + POOL_SLICE] = project(GA_END + pc, GA_END + pc + POOL_SLICE)
        else:
            gb_chunks.append(project(U_END + pc, U_END + pc + POOL_SLICE))
        ec = slice(n * EMB_COLS, (n + 1) * EMB_COLS)
        gate_pre = jnp.dot(xb, wgate_ref[:, ec], preferred_element_type=F32)
        emb = jnp.dot(pb, wple_ref[:, ec], preferred_element_type=F32)
        c_ref[rows, cols] = _group_outputs(scores, ga[rows, cols], v_group, kv_head,
                                           first_key, sinks_ref[heads], bias_ref[heads])
        ple_chunks.append(_sigmoid(gate_pre) * emb)

    gb = jnp.concatenate(gb_chunks, axis=1)
    pos = j * TILE + lax.broadcasted_iota(jnp.int32, (TILE, 1), 0)
    for grp, window in enumerate(POOL_WINDOWS):
        c0 = grp * POOL_GROUP_DIM
        cols = slice(c0, c0 + POOL_GROUP_DIM)
        total = u_ref[:, cols]
        shift = 1
        while shift < window:
            total = total + pltpu.roll(total, shift, axis=0)
            shift *= 2
        total = total[POOL_TAIL:]
        u = u_ref[POOL_TAIL:, cols]
        count = jnp.minimum(pos + 1, window).astype(F32)
        diff = total * (1.0 / count) - u
        mixed = jnp.dot(diff.astype(BF16), wpool_ref[grp], preferred_element_type=F32)
        g = gb[:, cols]
        pooled = mixed * pscale_ref[:, cols] * (g * _sigmoid(g))
        c_ref[:, ATTN_WIDTH + c0:ATTN_WIDTH + c0 + POOL_GROUP_DIM] = pooled.astype(c_ref.dtype)

    kv_ref[0:BLOCK, :] = kv_ref[TILE:TILE + BLOCK, :]
    u_ref[0:POOL_TAIL, :] = u_ref[TILE:TILE + POOL_TAIL, :]

    ple = jnp.concatenate(ple_chunks, axis=1)
    for blk in range(BLOCKS_PER_TILE):
        rows = slice(blk * BLOCK, (blk + 1) * BLOCK)
        mix = jnp.dot(c_ref[rows, :], wout_ref[:, :D_MODEL], preferred_element_type=F32)
        y = DEEPNORM_ALPHA * x_ref[rows, :] + mix + ple[rows]
        o_ref[rows, :] = _layer_norm(y, gain_ref[...], lnb_ref[...])

    if convert_next:
        next_weights.end_step()


def _layer(x3, p4, sinks, band_bias, layer_weights, next_f32, b_in, w_ple, w_pool, pool_scale,
           ln_gain, ln_bias, layer):
    batch, seq, _ = x3.shape
    steps = batch * (seq // TILE)
    convert_next = next_f32 is not None
    tile = lambda b, j: (b, j, 0)
    whole = lambda b, j: (0, 0)
    per_layer = lambda b, j: (layer, 0, 0)
    per_layer4 = lambda b, j: (layer, 0, 0, 0)
    in_hbm = pl.BlockSpec(memory_space=pl.ANY)
    x_shape = jax.ShapeDtypeStruct((batch, seq, D_MODEL), F32)
    x_spec = pl.BlockSpec((None, TILE, D_MODEL), tile)
    scratch = [pltpu.VMEM((BLOCK + TILE, 2 * KV_COLS), F32),
               pltpu.VMEM((POOL_TAIL + TILE, POOL_WIDTH), F32),
               pltpu.VMEM((TILE, D_MODEL), BF16)]
    if convert_next:
        assert all(w.shape[1] % steps == 0 for w in next_f32)
        slices = [(w.shape[1] // steps, w.shape[2]) for w in next_f32]
        out_shape = [x_shape] + [jax.ShapeDtypeStruct(w.shape, BF16) for w in layer_weights]
        out_specs = [x_spec] + [in_hbm] * len(next_f32)
        scratch += [pltpu.VMEM(s, F32) for s in slices]
        scratch += [pltpu.VMEM((rows, w.shape[1]), BF16)
                    for (rows, _), w in zip(slices, layer_weights)]
        scratch += [pltpu.SemaphoreType.DMA((2, len(next_f32)))]
    else:
        out_shape, out_specs = x_shape, x_spec
    outs = pl.pallas_call(
        functools.partial(_layer_kernel, convert_next, layer + 1, steps),
        grid=(batch, seq // TILE),
        in_specs=[
            x_spec,
            pl.BlockSpec((None, None, TILE, PLE_DIM), lambda b, j: (layer, b, j, 0)),
            _resident((D_MODEL, IN_COLS), whole),
            _resident((None, 1, IN_COLS), per_layer),
            _resident((D_MODEL, PADDED_COLS), whole),
            _resident((D_MODEL, PADDED_COLS), whole),
            _resident((None, PLE_DIM, PADDED_COLS), per_layer),
            _resident((None, len(POOL_WINDOWS), POOL_GROUP_DIM, POOL_GROUP_DIM), per_layer4),
            _resident((None, 1, POOL_WIDTH), per_layer),
            _resident((N_HEADS, BLOCK, 2 * BLOCK), lambda b, j: (0, 0, 0)),
            _resident((None, N_HEADS, 1, V7X_LANES), per_layer4),
            _resident((None, 1, D_MODEL), per_layer),
            _resident((None, 1, D_MODEL), per_layer),
        ] + ([in_hbm] * len(next_f32) if convert_next else []),
        out_specs=out_specs,
        out_shape=out_shape,
        scratch_shapes=scratch,
        compiler_params=pltpu.CompilerParams(
            dimension_semantics=("arbitrary", "arbitrary"),
            vmem_limit_bytes=V7X_VMEM_LIMIT_BYTES),
        name="hybrid_layer",
    )(x3, p4, layer_weights[0], b_in, layer_weights[1], layer_weights[2], w_ple, w_pool,
      pool_scale, band_bias, sinks, ln_gain, ln_bias, *(next_f32 if convert_next else ()))
    return (outs[0], tuple(outs[1:])) if convert_next else (outs, None)


def kernel(x, p, w_in, b_in, w_out, attn_sinks, rel_bias, w_pool, pool_scale, w_ple, w_gate_ple,
           ln_gain, ln_bias):
    batch, seq, _ = x.shape
    assert x.shape == (batch, seq, D_MODEL) and seq % TILE == 0
    assert w_in.shape == (DEPTH, D_MODEL, IN_COLS)

    layer_weights = (w_in[0].astype(BF16),
                     _to_padded_bf16(w_out, 1).reshape(D_MODEL, PADDED_COLS),
                     _to_padded_bf16(w_gate_ple, 1).reshape(D_MODEL, PADDED_COLS))
    next_f32 = (w_in, w_out, w_gate_ple)
    w_ple_b = _to_padded_bf16(w_ple, DEPTH)
    w_pool_b = w_pool.astype(BF16)
    b_in3 = b_in[:, None, :]
    pool_scale3 = pool_scale[:, None, :]
    ln_gain3 = ln_gain[:, None, :]
    ln_bias3 = ln_bias[:, None, :]

    order = jnp.array(_group_order(), jnp.int32)
    band_bias = _band_bias(rel_bias)[order]
    sinks = jnp.broadcast_to(attn_sinks[:, order, None, None], (DEPTH, N_HEADS, 1, V7X_LANES))
    for layer in range(DEPTH):
        x, layer_weights = _layer(x, p, sinks, band_bias, layer_weights,
                                  next_f32 if layer + 1 < DEPTH else None, b_in3, w_ple_b,
                                  w_pool_b, pool_scale3, ln_gain3, ln_bias3, layer)
    return x
```

```python
import functools
import math

import jax
import jax.numpy as jnp
import numpy as np
from jax import lax
from jax.experimental import pallas as pl
from jax.experimental.pallas import tpu as pltpu

D_MODEL = 2048
DEPTH = 4
PLE_DIM = 256
ATTN_WIDTH = D_MODEL // 2
POOL_WIDTH = D_MODEL - ATTN_WIDTH
HEAD_DIM = 64
N_HEADS = ATTN_WIDTH // HEAD_DIM
N_KV_HEADS = max(1, N_HEADS // 8)
KV_GROUP = N_HEADS // N_KV_HEADS
WINDOW = 128
BLOCK = WINDOW
POOL_WINDOWS = (2, 4, 8, 16)
POOL_GROUP_DIM = POOL_WIDTH // len(POOL_WINDOWS)
POOL_TAIL = 16
REL_BUCKETS = 32
REL_MAX_DIST = 128
LN_EPS = 1e-5
DEEPNORM_ALPHA = (2.0 * DEPTH) ** 0.25
Q_COLS = N_HEADS * HEAD_DIM
KV_COLS = N_KV_HEADS * HEAD_DIM
IN_COLS = Q_COLS + 2 * KV_COLS + ATTN_WIDTH + 2 * POOL_WIDTH
Q_END = Q_COLS
KV_END = Q_END + 2 * KV_COLS
GA_END = KV_END + ATTN_WIDTH
U_END = GA_END + POOL_WIDTH
ATTN_SCALE = 1.0 / math.sqrt(HEAD_DIM)
MASK_VALUE = -1e30

V7X_LANES = 128
PADDED_COLS = D_MODEL + V7X_LANES
V7X_VMEM_LIMIT_BYTES = 62 * 1024 * 1024

TILE = 2 * BLOCK
BLOCKS_PER_TILE = TILE // BLOCK
assert 2 * HEAD_DIM == V7X_LANES
PAIRS = KV_GROUP // 2
GROUP_COLS = KV_GROUP * HEAD_DIM
ATTN_GROUPS = BLOCKS_PER_TILE * N_KV_HEADS
EMB_COLS = D_MODEL // ATTN_GROUPS
POOL_SLICE = 2 * POOL_WIDTH // ATTN_GROUPS
assert POOL_WIDTH % POOL_SLICE == 0
PREP_ROWS = 1024
BF16 = jnp.bfloat16
F32 = jnp.float32


def _sigmoid(z):
    return 1.0 / (1.0 + jnp.exp(-z))


def _layer_norm(y, gain, bias):
    mu = jnp.mean(y, axis=-1, keepdims=True)
    yc = y - mu
    var = jnp.mean(yc * yc, axis=-1, keepdims=True)
    return yc * lax.rsqrt(var + LN_EPS) * gain + bias


def _resident(block_shape, index_map):
    return pl.BlockSpec(block_shape, index_map, pipeline_mode=pl.Buffered(1))


def _group_order():
    return [kv * KV_GROUP + 2 * pair + parity
            for kv in range(N_KV_HEADS) for parity in range(2) for pair in range(PAIRS)]


def _bias_kernel(rel_ref, bucket_ref, o_ref):
    bucket = bucket_ref[...]
    in_bucket = [bucket == b for b in range(REL_BUCKETS)]
    for slot, head in enumerate(_group_order()):
        acc = jnp.zeros(bucket.shape, F32)
        for b in range(REL_BUCKETS):
            acc = jnp.where(in_bucket[b], rel_ref[b, head], acc)
        o_ref[slot] = acc


def _band_bias(rel_bias):
    qq = np.arange(BLOCK)[:, None]
    kk = np.arange(2 * BLOCK)[None, :]
    dist = np.maximum(qq + BLOCK - kk, 0)
    max_exact = REL_BUCKETS // 2
    large = max_exact + np.floor(np.log(np.maximum(dist, 1) / max_exact)
                                 / math.log(REL_MAX_DIST / max_exact)
                                 * (REL_BUCKETS - max_exact)).astype(np.int32)
    large = np.minimum(large, REL_BUCKETS - 1)
    bucket = jnp.asarray(np.where(dist < max_exact, dist, large), jnp.int32)
    return pl.pallas_call(
        _bias_kernel,
        in_specs=[pl.BlockSpec(memory_space=pltpu.SMEM),
                  pl.BlockSpec(memory_space=pltpu.VMEM)],
        out_specs=pl.BlockSpec(memory_space=pltpu.VMEM),
        out_shape=jax.ShapeDtypeStruct((N_HEADS, BLOCK, 2 * BLOCK), F32),
        name="band_bias",
    )(rel_bias, bucket)


def _to_bf16_kernel(w_ref, o_ref):
    o_ref[:, :D_MODEL] = w_ref[...].astype(BF16)
    o_ref[:, D_MODEL:] = jnp.zeros((o_ref.shape[0], PADDED_COLS - D_MODEL), BF16)


def _to_padded_bf16(w, depth):
    _, rows, cols = w.shape
    block_rows = min(rows, PREP_ROWS)
    assert cols == D_MODEL and rows % block_rows == 0
    return pl.pallas_call(
        _to_bf16_kernel,
        grid=(depth, rows // block_rows),
        in_specs=[pl.BlockSpec((None, block_rows, D_MODEL), lambda l, r: (l, r, 0))],
        out_specs=pl.BlockSpec((None, block_rows, PADDED_COLS), lambda l, r: (l, r, 0)),
        out_shape=jax.ShapeDtypeStruct((depth, rows, PADDED_COLS), BF16),
        name="to_padded_bf16",
    )(w)


def _head_copies(group, kv_head):
    low_half = lax.broadcasted_iota(jnp.int32, group.shape, 1) < HEAD_DIM
    if kv_head == 0:
        low = jnp.where(low_half, group, 0.0)
        high = pltpu.roll(low, HEAD_DIM, axis=1)
    else:
        high = jnp.where(low_half, 0.0, group)
        low = pltpu.roll(high, HEAD_DIM, axis=1)
    return low, high


def _group_scores(q_group, k_group, kv_head):
    q_rows = jnp.concatenate([q_group[:, p * V7X_LANES:(p + 1) * V7X_LANES]
                              for p in range(PAIRS)], axis=0)
    contract_lanes = (((1,), (1,)), ((), ()))
    scores = [lax.dot_general(q_rows, k.astype(BF16), contract_lanes, preferred_element_type=F32)
              .reshape(PAIRS, BLOCK, 2 * BLOCK) for k in _head_copies(k_group, kv_head)]
    return jnp.concatenate(scores, axis=0)


def _group_outputs(scores, gate, v_group, kv_head, first_key, sinks, bias):
    qq = lax.broadcasted_iota(jnp.int32, (BLOCK, 2 * BLOCK), 0)
    kk = lax.broadcasted_iota(jnp.int32, (BLOCK, 2 * BLOCK), 1)
    dist = qq + BLOCK - kk
    valid = (dist >= 0) & (dist < WINDOW) & (kk >= first_key)
    s = jnp.where(valid[None], scores + bias, MASK_VALUE)
    sink = sinks[:, :, :1]
    m = jnp.maximum(jnp.max(s, axis=-1, keepdims=True), sink)
    e = jnp.exp(s - m).astype(BF16)
    sink_term = jnp.exp(sink - m)

    lane = lax.broadcasted_iota(jnp.int32, (2 * BLOCK, V7X_LANES), 1)
    ones = [jnp.where(lane < HEAD_DIM, 1.0, 0.0), jnp.where(lane < HEAD_DIM, 0.0, 1.0)]
    rows = PAIRS * BLOCK
    acc = None
    for parity, v in enumerate(_head_copies(v_group, kv_head)):
        v_wide = jnp.concatenate([v, ones[parity]], axis=1).astype(BF16)
        part = jnp.dot(e[parity * PAIRS:(parity + 1) * PAIRS].reshape(rows, 2 * BLOCK), v_wide,
                       preferred_element_type=F32)
        acc = part if acc is None else acc + part
    low_half = lax.broadcasted_iota(jnp.int32, (rows, V7X_LANES), 1) < HEAD_DIM
    denom = acc[:, V7X_LANES:] + jnp.where(low_half, sink_term[:PAIRS].reshape(rows, 1),
                                           sink_term[PAIRS:].reshape(rows, 1))
    attn = acc[:, :V7X_LANES] / denom
    attn = jnp.concatenate([attn[p * BLOCK:(p + 1) * BLOCK] for p in range(PAIRS)], axis=1)
    return (attn * (gate * _sigmoid(gate))).astype(BF16)


class _NextWeights:
    def __init__(self, layer, weights, sems, step, steps):
        self.layer, self.weights, self.sems, self.step, self.steps = layer, weights, sems, step, steps

    def _reads(self, slice_index):
        copies = []
        for n, (src, _, stage_in, _) in enumerate(self.weights):
            rows = stage_in.shape[0]
            copies.append(pltpu.make_async_copy(
                src.at[self.layer, pl.ds(slice_index * rows, rows), :], stage_in,
                self.sems.at[0, n]))
        return copies

    def _writes(self, slice_index):
        copies = []
        for n, (_, dst, _, stage_out) in enumerate(self.weights):
            rows = stage_out.shape[0]
            copies.append(pltpu.make_async_copy(
                stage_out, dst.at[pl.ds(slice_index * rows, rows), :], self.sems.at[1, n]))
        return copies

    def begin_step(self):
        step, last = self.step, self.steps - 1

        @pl.when(step == 0)
        def _():
            for _, _, _, stage_out in self.weights:
                stage_out[...] = jnp.zeros(stage_out.shape, stage_out.dtype)
            for copy in self._reads(0):
                copy.start()

        for copy in self._reads(step):
            copy.wait()

        @pl.when(step > 0)
        def _():
            for copy in self._writes(step - 1):
                copy.wait()

        for _, _, stage_in, stage_out in self.weights:
            stage_out[:, 0:stage_in.shape[1]] = stage_in[...].astype(BF16)
        for copy in self._writes(step):
            copy.start()
        for copy in self._reads(jnp.minimum(step + 1, last)):
            copy.start()

    def end_step(self):
        step, last = self.step, self.steps - 1

        @pl.when(step == last)
        def _():
            for copy in self._writes(last) + self._reads(last):
                copy.wait()


def _layer_kernel(convert_next, next_layer, steps, *refs):
    (x_ref, p_ref, win_ref, bin_ref, wout_ref, wgate_ref, wple_ref, wpool_ref, pscale_ref,
     bias_ref, sinks_ref, gain_ref, lnb_ref) = refs[:13]
    if convert_next:
        next_f32 = refs[13:16]
        o_ref = refs[16]
        next_bf16 = refs[17:20]
        kv_ref, u_ref, c_ref = refs[20:23]
        stage_in, stage_out, sems = refs[23:26], refs[26:29], refs[29]
    else:
        o_ref = refs[13]
        kv_ref, u_ref, c_ref = refs[14:17]
    j = pl.program_id(1)

    if convert_next:
        next_weights = _NextWeights(next_layer, list(zip(next_f32, next_bf16, stage_in, stage_out)),
                                    sems, pl.program_id(0) * pl.num_programs(1) + j, steps)
        next_weights.begin_step()

    @pl.when(j == 0)
    def _():
        kv_ref[0:BLOCK, :] = jnp.zeros((BLOCK, 2 * KV_COLS), F32)
        u_ref[0:POOL_TAIL, :] = jnp.zeros((POOL_TAIL, POOL_WIDTH), F32)

    xb = x_ref[...].astype(BF16)
    pb = p_ref[...].astype(BF16)

    def project(lo, hi):
        return (jnp.dot(xb, win_ref[:, lo:hi], preferred_element_type=F32) + bin_ref[:, lo:hi])

    q = (project(0, Q_END) * ATTN_SCALE).astype(BF16)
    kv_ref[BLOCK:, :] = project(Q_END, KV_END)
    ga = project(KV_END, GA_END)

    groups = []
    for blk in range(BLOCKS_PER_TILE):
        rows = slice(blk * BLOCK, (blk + 1) * BLOCK)
        first_key = jnp.where(j == 0, BLOCK, 0) if blk == 0 else 0
        kv = kv_ref[blk * BLOCK:(blk + 2) * BLOCK, :]
        for kv_head in range(N_KV_HEADS):
            cols = slice(kv_head * GROUP_COLS, (kv_head + 1) * GROUP_COLS)
            scores = _group_scores(q[rows, cols], kv[:, :V7X_LANES], kv_head)
            groups.append((rows, cols, kv_head, first_key, kv[:, V7X_LANES:], scores))
    ple_chunks = []
    gb_chunks = []
    for n, (rows, cols, kv_head, first_key, v_group, scores) in enumerate(groups):
        heads = slice(kv_head * KV_GROUP, (kv_head + 1) * KV_GROUP)
        pc = (n * POOL_SLICE) % POOL_WIDTH
        if n * POOL_SLICE < POOL_WIDTH:
            u_ref[POOL_TAIL:, pc:pc + POOL_SLICE] = project(GA_END + pc, GA_END + pc + POOL_SLICE)
        else:
            gb_chunks.append(project(U_END + pc, U_END + pc + POOL_SLICE))
        ec = slice(n * EMB_COLS, (n + 1) * EMB_COLS)
        gate_pre = jnp.dot(xb, wgate_ref[:, ec], preferred_element_type=F32)
        emb = jnp.dot(pb, wple_ref[:, ec], preferred_element_type=F32)
        c_ref[rows, cols] = _group_outputs(scores, ga[rows, cols], v_group, kv_head,
                                           first_key, sinks_ref[heads], bias_ref[heads])
        ple_chunks.append(_sigmoid(gate_pre) * emb)

    gb = jnp.concatenate(gb_chunks, axis=1)
    pos = j * TILE + lax.broadcasted_iota(jnp.int32, (TILE, 1), 0)
    for grp, window in enumerate(POOL_WINDOWS):
        c0 = grp * POOL_GROUP_DIM
        cols = slice(c0, c0 + POOL_GROUP_DIM)
        total = u_ref[:, cols]
        shift = 1
        while shift < window:
            total = total + pltpu.roll(total, shift, axis=0)
            shift *= 2
        total = total[POOL_TAIL:]
        u = u_ref[POOL_TAIL:, cols]
        count = jnp.minimum(pos + 1, window).astype(F32)
        diff = total * (1.0 / count) - u
        mixed = jnp.dot(diff.astype(BF16), wpool_ref[grp], preferred_element_type=F32)
        g = gb[:, cols]
        pooled = mixed * pscale_ref[:, cols] * (g * _sigmoid(g))
        c_ref[:, ATTN_WIDTH + c0:ATTN_WIDTH + c0 + POOL_GROUP_DIM] = pooled.astype(c_ref.dtype)

    kv_ref[0:BLOCK, :] = kv_ref[TILE:TILE + BLOCK, :]
    u_ref[0:POOL_TAIL, :] = u_ref[TILE:TILE + POOL_TAIL, :]

    ple = jnp.concatenate(ple_chunks, axis=1)
    for blk in range(BLOCKS_PER_TILE):
        rows = slice(blk * BLOCK, (blk + 1) * BLOCK)
        mix = jnp.dot(c_ref[rows, :], wout_ref[:, :D_MODEL], preferred_element_type=F32)
        y = DEEPNORM_ALPHA * x_ref[rows, :] + mix + ple[rows]
        o_ref[rows, :] = _layer_norm(y, gain_ref[...], lnb_ref[...])

    if convert_next:
        next_weights.end_step()


def _layer(x3, p4, sinks, band_bias, layer_weights, next_f32, b_in, w_ple, w_pool, pool_scale,
           ln_gain, ln_bias, layer):
    batch, seq, _ = x3.shape
    steps = batch * (seq // TILE)
    convert_next = next_f32 is not None
    tile = lambda b, j: (b, j, 0)
    whole = lambda b, j: (0, 0)
    per_layer = lambda b, j: (layer, 0, 0)
    per_layer4 = lambda b, j: (layer, 0, 0, 0)
    in_hbm = pl.BlockSpec(memory_space=pl.ANY)
    x_shape = jax.ShapeDtypeStruct((batch, seq, D_MODEL), F32)
    x_spec = pl.BlockSpec((None, TILE, D_MODEL), tile)
    scratch = [pltpu.VMEM((BLOCK + TILE, 2 * KV_COLS), F32),
               pltpu.VMEM((POOL_TAIL + TILE, POOL_WIDTH), F32),
               pltpu.VMEM((TILE, D_MODEL), BF16)]
    if convert_next:
        assert all(w.shape[1] % steps == 0 for w in next_f32)
        slices = [(w.shape[1] // steps, w.shape[2]) for w in next_f32]
        out_shape = [x_shape] + [jax.ShapeDtypeStruct(w.shape, BF16) for w in layer_weights]
        out_specs = [x_spec] + [in_hbm] * len(next_f32)
        scratch += [pltpu.VMEM(s, F32) for s in slices]
        scratch += [pltpu.VMEM((rows, w.shape[1]), BF16)
                    for (rows, _), w in zip(slices, layer_weights)]
        scratch += [pltpu.SemaphoreType.DMA((2, len(next_f32)))]
    else:
        out_shape, out_specs = x_shape, x_spec
    outs = pl.pallas_call(
        functools.partial(_layer_kernel, convert_next, layer + 1, steps),
        grid=(batch, seq // TILE),
        in_specs=[
            x_spec,
            pl.BlockSpec((None, None, TILE, PLE_DIM), lambda b, j: (layer, b, j, 0)),
            _resident((D_MODEL, IN_COLS), whole),
            _resident((None, 1, IN_COLS), per_layer),
            _resident((D_MODEL, PADDED_COLS), whole),
            _resident((D_MODEL, PADDED_COLS), whole),
            _resident((None, PLE_DIM, PADDED_COLS), per_layer),
            _resident((None, len(POOL_WINDOWS), POOL_GROUP_DIM, POOL_GROUP_DIM), per_layer4),
            _resident((None, 1, POOL_WIDTH), per_layer),
            _resident((N_HEADS, BLOCK, 2 * BLOCK), lambda b, j: (0, 0, 0)),
            _resident((None, N_HEADS, 1, V7X_LANES), per_layer4),
            _resident((None, 1, D_MODEL), per_layer),
            _resident((None, 1, D_MODEL), per_layer),
        ] + ([in_hbm] * len(next_f32) if convert_next else []),
        out_specs=out_specs,
        out_shape=out_shape,
        scratch_shapes=scratch,
        compiler_params=pltpu.CompilerParams(
            dimension_semantics=("arbitrary", "arbitrary"),
            vmem_limit_bytes=V7X_VMEM_LIMIT_BYTES),
        name="hybrid_layer",
    )(x3, p4, layer_weights[0], b_in, layer_weights[1], layer_weights[2], w_ple, w_pool,
      pool_scale, band_bias, sinks, ln_gain, ln_bias, *(next_f32 if convert_next else ()))
    return (outs[0], tuple(outs[1:])) if convert_next else (outs, None)


def kernel(x, p, w_in, b_in, w_out, attn_sinks, rel_bias, w_pool, pool_scale, w_ple, w_gate_ple,
           ln_gain, ln_bias):
    batch, seq, _ = x.shape
    assert x.shape == (batch, seq, D_MODEL) and seq % TILE == 0
    assert w_in.shape == (DEPTH, D_MODEL, IN_COLS)

    layer_weights = (w_in[0].astype(BF16),
                     _to_padded_bf16(w_out, 1).reshape(D_MODEL, PADDED_COLS),
                     _to_padded_bf16(w_gate_ple, 1).reshape(D_MODEL, PADDED_COLS))
    next_f32 = (w_in, w_out, w_gate_ple)
    w_ple_b = _to_padded_bf16(w_ple, DEPTH)
    w_pool_b = w_pool.astype(BF16)
    b_in3 = b_in[:, None, :]
    pool_scale3 = pool_scale[:, None, :]
    ln_gain3 = ln_gain[:, None, :]
    ln_bias3 = ln_bias[:, None, :]

    order = jnp.array(_group_order(), jnp.int32)
    band_bias = _band_bias(rel_bias)
    sinks = jnp.broadcast_to(attn_sinks[:, order, None, None], (DEPTH, N_HEADS, 1, V7X_LANES))
    for layer in range(DEPTH):
        x, layer_weights = _layer(x, p, sinks, band_bias, layer_weights,
                                  next_f32 if layer + 1 < DEPTH else None, b_in3, w_ple_b,
                                  w_pool_b, pool_scale3, ln_gain3, ln_bias3, layer)
    return x
```

```python
import functools
import math

import jax
import jax.numpy as jnp
import numpy as np
from jax import lax
from jax.experimental import pallas as pl
from jax.experimental.pallas import tpu as pltpu

D_MODEL = 2048
DEPTH = 4
PLE_DIM = 256
ATTN_WIDTH = D_MODEL // 2
POOL_WIDTH = D_MODEL - ATTN_WIDTH
HEAD_DIM = 64
N_HEADS = ATTN_WIDTH // HEAD_DIM
N_KV_HEADS = max(1, N_HEADS // 8)
KV_GROUP = N_HEADS // N_KV_HEADS
WINDOW = 128
BLOCK = WINDOW
POOL_WINDOWS = (2, 4, 8, 16)
POOL_GROUP_DIM = POOL_WIDTH // len(POOL_WINDOWS)
POOL_TAIL = 16
REL_BUCKETS = 32
REL_MAX_DIST = 128
LN_EPS = 1e-5
DEEPNORM_ALPHA = (2.0 * DEPTH) ** 0.25
Q_COLS = N_HEADS * HEAD_DIM
KV_COLS = N_KV_HEADS * HEAD_DIM
IN_COLS = Q_COLS + 2 * KV_COLS + ATTN_WIDTH + 2 * POOL_WIDTH
Q_END = Q_COLS
KV_END = Q_END + 2 * KV_COLS
GA_END = KV_END + ATTN_WIDTH
U_END = GA_END + POOL_WIDTH
ATTN_SCALE = 1.0 / math.sqrt(HEAD_DIM)
MASK_VALUE = -1e30

V7X_LANES = 128
PADDED_COLS = D_MODEL + V7X_LANES
V7X_VMEM_LIMIT_BYTES = 62 * 1024 * 1024

TILE = 2 * BLOCK
BLOCKS_PER_TILE = TILE // BLOCK
assert 2 * HEAD_DIM == V7X_LANES
PAIRS = KV_GROUP // 2
GROUP_COLS = KV_GROUP * HEAD_DIM
ATTN_GROUPS = BLOCKS_PER_TILE * N_KV_HEADS
EMB_COLS = D_MODEL // ATTN_GROUPS
POOL_SLICE = 2 * POOL_WIDTH // ATTN_GROUPS
assert POOL_WIDTH % POOL_SLICE == 0
PREP_ROWS = 1024
BF16 = jnp.bfloat16
F32 = jnp.float32


def _sigmoid(z):
    return 1.0 / (1.0 + jnp.exp(-z))


def _layer_norm(y, gain, bias):
    mu = jnp.mean(y, axis=-1, keepdims=True)
    yc = y - mu
    var = jnp.mean(yc * yc, axis=-1, keepdims=True)
    return yc * lax.rsqrt(var + LN_EPS) * gain + bias


def _resident(block_shape, index_map):
    return pl.BlockSpec(block_shape, index_map, pipeline_mode=pl.Buffered(1))


def _group_order():
    return [kv * KV_GROUP + 2 * pair + parity
            for kv in range(N_KV_HEADS) for parity in range(2) for pair in range(PAIRS)]


def _bias_kernel(rel_ref, bucket_ref, o_ref):
    bucket = bucket_ref[...]
    in_bucket = [bucket == b for b in range(REL_BUCKETS)]
    for slot, head in enumerate(_group_order()):
        acc = jnp.zeros(bucket.shape, F32)
        for b in range(REL_BUCKETS):
            acc = jnp.where(in_bucket[b], rel_ref[b, head], acc)
        o_ref[slot] = acc


def _band_bias(rel_bias):
    qq = np.arange(BLOCK)[:, None]
    kk = np.arange(2 * BLOCK)[None, :]
    dist = np.maximum(qq + BLOCK - kk, 0)
    max_exact = REL_BUCKETS // 2
    large = max_exact + np.floor(np.log(np.maximum(dist, 1) / max_exact)
                                 / math.log(REL_MAX_DIST / max_exact)
                                 * (REL_BUCKETS - max_exact)).astype(np.int32)
    large = np.minimum(large, REL_BUCKETS - 1)
    bucket = jnp.asarray(np.where(dist < max_exact, dist, large), jnp.int32)
    return pl.pallas_call(
        _bias_kernel,
        in_specs=[pl.BlockSpec(memory_space=pltpu.SMEM),
                  pl.BlockSpec(memory_space=pltpu.VMEM)],
        out_specs=pl.BlockSpec(memory_space=pltpu.VMEM),
        out_shape=jax.ShapeDtypeStruct((N_HEADS, BLOCK, 2 * BLOCK), F32),
        name="band_bias",
    )(rel_bias, bucket)


def _to_bf16_kernel(w_ref, o_ref):
    o_ref[:, :D_MODEL] = w_ref[...].astype(BF16)
    o_ref[:, D_MODEL:] = jnp.zeros((o_ref.shape[0], PADDED_COLS - D_MODEL), BF16)


def _to_padded_bf16(w, depth):
    _, rows, cols = w.shape
    block_rows = min(rows, PREP_ROWS)
    assert cols == D_MODEL and rows % block_rows == 0
    return pl.pallas_call(
        _to_bf16_kernel,
        grid=(depth, rows // block_rows),
        in_specs=[pl.BlockSpec((None, block_rows, D_MODEL), lambda l, r: (l, r, 0))],
        out_specs=pl.BlockSpec((None, block_rows, PADDED_COLS), lambda l, r: (l, r, 0)),
        out_shape=jax.ShapeDtypeStruct((depth, rows, PADDED_COLS), BF16),
        name="to_padded_bf16",
    )(w)


def _head_copies(group, kv_head):
    low_half = lax.broadcasted_iota(jnp.int32, group.shape, 1) < HEAD_DIM
    if kv_head == 0:
        low = jnp.where(low_half, group, 0.0)
        high = pltpu.roll(low, HEAD_DIM, axis=1)
    else:
        high = jnp.where(low_half, 0.0, group)
        low = pltpu.roll(high, HEAD_DIM, axis=1)
    return low, high


def _group_scores(q_group, k_group, kv_head):
    q_rows = jnp.concatenate([q_group[:, p * V7X_LANES:(p + 1) * V7X_LANES]
                              for p in range(PAIRS)], axis=0)
    contract_lanes = (((1,), (1,)), ((), ()))
    scores = [lax.dot_general(q_rows, k.astype(BF16), contract_lanes, preferred_element_type=F32)
              .reshape(PAIRS, BLOCK, 2 * BLOCK) for k in _head_copies(k_group, kv_head)]
    return jnp.concatenate(scores, axis=0)


def _group_outputs(scores, gate, v_group, kv_head, first_key, sinks, bias):
    qq = lax.broadcasted_iota(jnp.int32, (BLOCK, 2 * BLOCK), 0)
    kk = lax.broadcasted_iota(jnp.int32, (BLOCK, 2 * BLOCK), 1)
    dist = qq + BLOCK - kk
    valid = (dist >= 0) & (dist < WINDOW) & (kk >= first_key)
    s = jnp.where(valid[None], scores + bias, MASK_VALUE)
    sink = sinks[:, :, :1]
    m = jnp.maximum(jnp.max(s, axis=-1, keepdims=True), sink)
    e = jnp.exp(s - m).astype(BF16)
    sink_term = jnp.exp(sink - m)

    lane = lax.broadcasted_iota(jnp.int32, (2 * BLOCK, V7X_LANES), 1)
    ones = [jnp.where(lane < HEAD_DIM, 1.0, 0.0), jnp.where(lane < HEAD_DIM, 0.0, 1.0)]
    rows = PAIRS * BLOCK
    acc = None
    for parity, v in enumerate(_head_copies(v_group, kv_head)):
        v_wide = jnp.concatenate([v, ones[parity]], axis=1).astype(BF16)
        part = jnp.dot(e[parity * PAIRS:(parity + 1) * PAIRS].reshape(rows, 2 * BLOCK), v_wide,
                       preferred_element_type=F32)
        acc = part if acc is None else acc + part
    low_half = lax.broadcasted_iota(jnp.int32, (rows, V7X_LANES), 1) < HEAD_DIM
    denom = acc[:, V7X_LANES:] + jnp.where(low_half, sink_term[:PAIRS].reshape(rows, 1),
                                           sink_term[PAIRS:].reshape(rows, 1))
    attn = acc[:, :V7X_LANES] / denom
    attn = jnp.concatenate([attn[p * BLOCK:(p + 1) * BLOCK] for p in range(PAIRS)], axis=1)
    return (attn * (gate * _sigmoid(gate))).astype(BF16)


class _NextWeights:
    def __init__(self, layer, weights, sems, step, steps):
        self.layer, self.weights, self.sems, self.step, self.steps = layer, weights, sems, step, steps

    def _reads(self, slice_index):
        copies = []
        for n, (src, _, stage_in, _) in enumerate(self.weights):
            rows = stage_in.shape[0]
            copies.append(pltpu.make_async_copy(
                src.at[self.layer, pl.ds(slice_index * rows, rows), :], stage_in,
                self.sems.at[0, n]))
        return copies

    def _writes(self, slice_index):
        copies = []
        for n, (_, dst, _, stage_out) in enumerate(self.weights):
            rows = stage_out.shape[0]
            copies.append(pltpu.make_async_copy(
                stage_out, dst.at[pl.ds(slice_index * rows, rows), :], self.sems.at[1, n]))
        return copies

    def begin_step(self):
        step, last = self.step, self.steps - 1

        @pl.when(step == 0)
        def _():
            for _, _, _, stage_out in self.weights:
                stage_out[...] = jnp.zeros(stage_out.shape, stage_out.dtype)
            for copy in self._reads(0):
                copy.start()

        for copy in self._reads(step):
            copy.wait()

        @pl.when(step > 0)
        def _():
            for copy in self._writes(step - 1):
                copy.wait()

        for _, _, stage_in, stage_out in self.weights:
            stage_out[:, 0:stage_in.shape[1]] = stage_in[...].astype(BF16)
        for copy in self._writes(step):
            copy.start()
        for copy in self._reads(jnp.minimum(step + 1, last)):
            copy.start()

    def end_step(self):
        step, last = self.step, self.steps - 1

        @pl.when(step == last)
        def _():
            for copy in self._writes(last) + self._reads(last):
                copy.wait()


def _layer_kernel(convert_next, layer, steps, *refs):
    (x_ref, p_ref, win_ref, bin_all, wout_ref, wgate_ref, wple_ref, wpool_ref, pscale_all,
     bias_ref, sinks_ref, gain_all, lnb_all) = refs[:13]
    bin_ref, pscale_ref, gain_ref, lnb_ref = (r.at[layer:layer + 1]
                                              for r in (bin_all, pscale_all, gain_all, lnb_all))
    next_layer = layer + 1
    if convert_next:
        next_f32 = refs[13:16]
        o_ref = refs[16]
        next_bf16 = refs[17:20]
        kv_ref, u_ref, c_ref = refs[20:23]
        stage_in, stage_out, sems = refs[23:26], refs[26:29], refs[29]
    else:
        o_ref = refs[13]
        kv_ref, u_ref, c_ref = refs[14:17]
    j = pl.program_id(1)

    if convert_next:
        next_weights = _NextWeights(next_layer, list(zip(next_f32, next_bf16, stage_in, stage_out)),
                                    sems, pl.program_id(0) * pl.num_programs(1) + j, steps)
        next_weights.begin_step()

    @pl.when(j == 0)
    def _():
        kv_ref[0:BLOCK, :] = jnp.zeros((BLOCK, 2 * KV_COLS), F32)
        u_ref[0:POOL_TAIL, :] = jnp.zeros((POOL_TAIL, POOL_WIDTH), F32)

    xb = x_ref[...].astype(BF16)
    pb = p_ref[...].astype(BF16)

    def project(lo, hi):
        return (jnp.dot(xb, win_ref[:, lo:hi], preferred_element_type=F32) + bin_ref[:, lo:hi])

    q = (project(0, Q_END) * ATTN_SCALE).astype(BF16)
    kv_ref[BLOCK:, :] = project(Q_END, KV_END)
    ga = project(KV_END, GA_END)

    groups = []
    for blk in range(BLOCKS_PER_TILE):
        rows = slice(blk * BLOCK, (blk + 1) * BLOCK)
        first_key = jnp.where(j == 0, BLOCK, 0) if blk == 0 else 0
        kv = kv_ref[blk * BLOCK:(blk + 2) * BLOCK, :]
        for kv_head in range(N_KV_HEADS):
            cols = slice(kv_head * GROUP_COLS, (kv_head + 1) * GROUP_COLS)
            scores = _group_scores(q[rows, cols], kv[:, :V7X_LANES], kv_head)
            groups.append((rows, cols, kv_head, first_key, kv[:, V7X_LANES:], scores))
    ple_chunks = []
    gb_chunks = []
    for n, (rows, cols, kv_head, first_key, v_group, scores) in enumerate(groups):
        heads = slice(kv_head * KV_GROUP, (kv_head + 1) * KV_GROUP)
        pc = (n * POOL_SLICE) % POOL_WIDTH
        if n * POOL_SLICE < POOL_WIDTH:
            u_ref[POOL_TAIL:, pc:pc + POOL_SLICE] = project(GA_END + pc, GA_END + pc + POOL_SLICE)
        else:
            gb_chunks.append(project(U_END + pc, U_END + pc + POOL_SLICE))
        ec = slice(n * EMB_COLS, (n + 1) * EMB_COLS)
        gate_pre = jnp.dot(xb, wgate_ref[:, ec], preferred_element_type=F32)
        emb = jnp.dot(pb, wple_ref[:, ec], preferred_element_type=F32)
        c_ref[rows, cols] = _group_outputs(scores, ga[rows, cols], v_group, kv_head,
                                           first_key, sinks_ref[heads], bias_ref[heads])
        ple_chunks.append(_sigmoid(gate_pre) * emb)

    gb = jnp.concatenate(gb_chunks, axis=1)
    pos = j * TILE + lax.broadcasted_iota(jnp.int32, (TILE, 1), 0)
    for grp, window in enumerate(POOL_WINDOWS):
        c0 = grp * POOL_GROUP_DIM
        cols = slice(c0, c0 + POOL_GROUP_DIM)
        total = u_ref[:, cols]
        shift = 1
        while shift < window:
            total = total + pltpu.roll(total, shift, axis=0)
            shift *= 2
        total = total[POOL_TAIL:]
        u = u_ref[POOL_TAIL:, cols]
        count = jnp.minimum(pos + 1, window).astype(F32)
        diff = total * (1.0 / count) - u
        mixed = jnp.dot(diff.astype(BF16), wpool_ref[grp], preferred_element_type=F32)
        g = gb[:, cols]
        pooled = mixed * pscale_ref[:, cols] * (g * _sigmoid(g))
        c_ref[:, ATTN_WIDTH + c0:ATTN_WIDTH + c0 + POOL_GROUP_DIM] = pooled.astype(c_ref.dtype)

    kv_ref[0:BLOCK, :] = kv_ref[TILE:TILE + BLOCK, :]
    u_ref[0:POOL_TAIL, :] = u_ref[TILE:TILE + POOL_TAIL, :]

    ple = jnp.concatenate(ple_chunks, axis=1)
    for blk in range(BLOCKS_PER_TILE):
        rows = slice(blk * BLOCK, (blk + 1) * BLOCK)
        mix = jnp.dot(c_ref[rows, :], wout_ref[:, :D_MODEL], preferred_element_type=F32)
        y = DEEPNORM_ALPHA * x_ref[rows, :] + mix + ple[rows]
        o_ref[rows, :] = _layer_norm(y, gain_ref[...], lnb_ref[...])

    if convert_next:
        next_weights.end_step()


def _layer(x3, p4, sinks, band_bias, layer_weights, next_f32, b_in, w_ple, w_pool, pool_scale,
           ln_gain, ln_bias, layer):
    batch, seq, _ = x3.shape
    steps = batch * (seq // TILE)
    convert_next = next_f32 is not None
    tile = lambda b, j: (b, j, 0)
    whole = lambda b, j: (0, 0)
    per_layer = lambda b, j: (layer, 0, 0)
    per_layer4 = lambda b, j: (layer, 0, 0, 0)
    in_hbm = pl.BlockSpec(memory_space=pl.ANY)
    x_shape = jax.ShapeDtypeStruct((batch, seq, D_MODEL), F32)
    x_spec = pl.BlockSpec((None, TILE, D_MODEL), tile)
    scratch = [pltpu.VMEM((BLOCK + TILE, 2 * KV_COLS), F32),
               pltpu.VMEM((POOL_TAIL + TILE, POOL_WIDTH), F32),
               pltpu.VMEM((TILE, D_MODEL), BF16)]
    if convert_next:
        assert all(w.shape[1] % steps == 0 for w in next_f32)
        slices = [(w.shape[1] // steps, w.shape[2]) for w in next_f32]
        out_shape = [x_shape] + [jax.ShapeDtypeStruct(w.shape, BF16) for w in layer_weights]
        out_specs = [x_spec] + [in_hbm] * len(next_f32)
        scratch += [pltpu.VMEM(s, F32) for s in slices]
        scratch += [pltpu.VMEM((rows, w.shape[1]), BF16)
                    for (rows, _), w in zip(slices, layer_weights)]
        scratch += [pltpu.SemaphoreType.DMA((2, len(next_f32)))]
    else:
        out_shape, out_specs = x_shape, x_spec
    outs = pl.pallas_call(
        functools.partial(_layer_kernel, convert_next, layer, steps),
        grid=(batch, seq // TILE),
        in_specs=[
            x_spec,
            pl.BlockSpec((None, None, TILE, PLE_DIM), lambda b, j: (layer, b, j, 0)),
            _resident((D_MODEL, IN_COLS), whole),
            _resident((DEPTH, IN_COLS), whole),
            _resident((D_MODEL, PADDED_COLS), whole),
            _resident((D_MODEL, PADDED_COLS), whole),
            _resident((None, PLE_DIM, PADDED_COLS), per_layer),
            _resident((None, len(POOL_WINDOWS), POOL_GROUP_DIM, POOL_GROUP_DIM), per_layer4),
            _resident((DEPTH, POOL_WIDTH), whole),
            _resident((N_HEADS, BLOCK, 2 * BLOCK), lambda b, j: (0, 0, 0)),
            _resident((None, N_HEADS, 1, V7X_LANES), per_layer4),
            _resident((DEPTH, D_MODEL), whole),
            _resident((DEPTH, D_MODEL), whole),
        ] + ([in_hbm] * len(next_f32) if convert_next else []),
        out_specs=out_specs,
        out_shape=out_shape,
        scratch_shapes=scratch,
        compiler_params=pltpu.CompilerParams(
            dimension_semantics=("arbitrary", "arbitrary"),
            vmem_limit_bytes=V7X_VMEM_LIMIT_BYTES),
        name="hybrid_layer",
    )(x3, p4, layer_weights[0], b_in, layer_weights[1], layer_weights[2], w_ple, w_pool,
      pool_scale, band_bias, sinks, ln_gain, ln_bias, *(next_f32 if convert_next else ()))
    return (outs[0], tuple(outs[1:])) if convert_next else (outs, None)


def kernel(x, p, w_in, b_in, w_out, attn_sinks, rel_bias, w_pool, pool_scale, w_ple, w_gate_ple,
           ln_gain, ln_bias):
    batch, seq, _ = x.shape
    assert x.shape == (batch, seq, D_MODEL) and seq % TILE == 0
    assert w_in.shape == (DEPTH, D_MODEL, IN_COLS)

    layer_weights = (w_in[0].astype(BF16),
                     _to_padded_bf16(w_out, 1).reshape(D_MODEL, PADDED_COLS),
                     _to_padded_bf16(w_gate_ple, 1).reshape(D_MODEL, PADDED_COLS))
    next_f32 = (w_in, w_out, w_gate_ple)
    w_ple_b = _to_padded_bf16(w_ple, DEPTH)
    w_pool_b = w_pool.astype(BF16)

    order = jnp.array(_group_order(), jnp.int32)
    band_bias = _band_bias(rel_bias)
    sinks = jnp.broadcast_to(attn_sinks[:, order, None, None], (DEPTH, N_HEADS, 1, V7X_LANES))
    for layer in range(DEPTH):
        x, layer_weights = _layer(x, p, sinks, band_bias, layer_weights,
                                  next_f32 if layer + 1 < DEPTH else None, b_in, w_ple_b,
                                  w_pool_b, pool_scale, ln_gain, ln_bias, layer)
    return x
```

```python
import functools
import math

import jax
import jax.numpy as jnp
import numpy as np
from jax import lax
from jax.experimental import pallas as pl
from jax.experimental.pallas import tpu as pltpu

D_MODEL = 2048
DEPTH = 4
PLE_DIM = 256
ATTN_WIDTH = D_MODEL // 2
POOL_WIDTH = D_MODEL - ATTN_WIDTH
HEAD_DIM = 64
N_HEADS = ATTN_WIDTH // HEAD_DIM
N_KV_HEADS = max(1, N_HEADS // 8)
KV_GROUP = N_HEADS // N_KV_HEADS
WINDOW = 128
BLOCK = WINDOW
POOL_WINDOWS = (2, 4, 8, 16)
POOL_GROUP_DIM = POOL_WIDTH // len(POOL_WINDOWS)
POOL_TAIL = 16
REL_BUCKETS = 32
REL_MAX_DIST = 128
LN_EPS = 1e-5
DEEPNORM_ALPHA = (2.0 * DEPTH) ** 0.25
Q_COLS = N_HEADS * HEAD_DIM
KV_COLS = N_KV_HEADS * HEAD_DIM
IN_COLS = Q_COLS + 2 * KV_COLS + ATTN_WIDTH + 2 * POOL_WIDTH
Q_END = Q_COLS
KV_END = Q_END + 2 * KV_COLS
GA_END = KV_END + ATTN_WIDTH
U_END = GA_END + POOL_WIDTH
ATTN_SCALE = 1.0 / math.sqrt(HEAD_DIM)
MASK_VALUE = -1e30

V7X_LANES = 128
PADDED_COLS = D_MODEL + V7X_LANES
V7X_VMEM_LIMIT_BYTES = 62 * 1024 * 1024

TILE = 2 * BLOCK
BLOCKS_PER_TILE = TILE // BLOCK
assert 2 * HEAD_DIM == V7X_LANES
PAIRS = KV_GROUP // 2
GROUP_COLS = KV_GROUP * HEAD_DIM
ATTN_GROUPS = BLOCKS_PER_TILE * N_KV_HEADS
EMB_COLS = D_MODEL // ATTN_GROUPS
POOL_SLICE = 2 * POOL_WIDTH // ATTN_GROUPS
assert POOL_WIDTH % POOL_SLICE == 0
PREP_ROWS = 1024
BF16 = jnp.bfloat16
F32 = jnp.float32


def _sigmoid(z):
    return 1.0 / (1.0 + jnp.exp(-z))


def _layer_norm(y, gain, bias):
    mu = jnp.mean(y, axis=-1, keepdims=True)
    yc = y - mu
    var = jnp.mean(yc * yc, axis=-1, keepdims=True)
    return yc * lax.rsqrt(var + LN_EPS) * gain + bias


def _resident(block_shape, index_map):
    return pl.BlockSpec(block_shape, index_map, pipeline_mode=pl.Buffered(1))


def _group_order():
    return [kv * KV_GROUP + 2 * pair + parity
            for kv in range(N_KV_HEADS) for parity in range(2) for pair in range(PAIRS)]


def _bias_kernel(rel_ref, bucket_ref, o_ref):
    bucket = bucket_ref[...]
    in_bucket = [bucket == b for b in range(REL_BUCKETS)]
    for slot, head in enumerate(_group_order()):
        acc = jnp.zeros(bucket.shape, F32)
        for b in range(REL_BUCKETS):
            acc = jnp.where(in_bucket[b], rel_ref[b, head], acc)
        o_ref[slot] = acc


def _band_bias(rel_bias):
    qq = np.arange(BLOCK)[:, None]
    kk = np.arange(2 * BLOCK)[None, :]
    dist = np.maximum(qq + BLOCK - kk, 0)
    max_exact = REL_BUCKETS // 2
    large = max_exact + np.floor(np.log(np.maximum(dist, 1) / max_exact)
                                 / math.log(REL_MAX_DIST / max_exact)
                                 * (REL_BUCKETS - max_exact)).astype(np.int32)
    large = np.minimum(large, REL_BUCKETS - 1)
    bucket = jnp.asarray(np.where(dist < max_exact, dist, large), jnp.int32)
    return pl.pallas_call(
        _bias_kernel,
        in_specs=[pl.BlockSpec(memory_space=pltpu.SMEM),
                  pl.BlockSpec(memory_space=pltpu.VMEM)],
        out_specs=pl.BlockSpec(memory_space=pltpu.VMEM),
        out_shape=jax.ShapeDtypeStruct((N_HEADS, BLOCK, 2 * BLOCK), F32),
        name="band_bias",
    )(rel_bias, bucket)


def _to_bf16_kernel(w_ref, o_ref):
    o_ref[:, :D_MODEL] = w_ref[...].astype(BF16)
    o_ref[:, D_MODEL:] = jnp.zeros((o_ref.shape[0], PADDED_COLS - D_MODEL), BF16)


def _to_padded_bf16(w, depth):
    _, rows, cols = w.shape
    block_rows = min(rows, PREP_ROWS)
    assert cols == D_MODEL and rows % block_rows == 0
    return pl.pallas_call(
        _to_bf16_kernel,
        grid=(depth, rows // block_rows),
        in_specs=[pl.BlockSpec((None, block_rows, D_MODEL), lambda l, r: (l, r, 0))],
        out_specs=pl.BlockSpec((None, block_rows, PADDED_COLS), lambda l, r: (l, r, 0)),
        out_shape=jax.ShapeDtypeStruct((depth, rows, PADDED_COLS), BF16),
        name="to_padded_bf16",
    )(w)


def _head_copies(group, kv_head):
    low_half = lax.broadcasted_iota(jnp.int32, group.shape, 1) < HEAD_DIM
    if kv_head == 0:
        low = jnp.where(low_half, group, 0.0)
        high = pltpu.roll(low, HEAD_DIM, axis=1)
    else:
        high = jnp.where(low_half, 0.0, group)
        low = pltpu.roll(high, HEAD_DIM, axis=1)
    return low, high


def _group_scores(q_group, k_group, kv_head):
    q_rows = jnp.concatenate([q_group[:, p * V7X_LANES:(p + 1) * V7X_LANES]
                              for p in range(PAIRS)], axis=0)
    contract_lanes = (((1,), (1,)), ((), ()))
    scores = [lax.dot_general(q_rows, k.astype(BF16), contract_lanes, preferred_element_type=F32)
              .reshape(PAIRS, BLOCK, 2 * BLOCK) for k in _head_copies(k_group, kv_head)]
    return jnp.concatenate(scores, axis=0)


def _group_outputs(scores, gate, v_group, kv_head, first_key, sinks, bias):
    qq = lax.broadcasted_iota(jnp.int32, (BLOCK, 2 * BLOCK), 0)
    kk = lax.broadcasted_iota(jnp.int32, (BLOCK, 2 * BLOCK), 1)
    dist = qq + BLOCK - kk
    valid = (dist >= 0) & (dist < WINDOW) & (kk >= first_key)
    s = jnp.where(valid[None], scores + bias, MASK_VALUE)
    sink = sinks[:, :, :1]
    m = jnp.maximum(jnp.max(s, axis=-1, keepdims=True), sink)
    e = jnp.exp(s - m).astype(BF16)
    sink_term = jnp.exp(sink - m)

    lane = lax.broadcasted_iota(jnp.int32, (2 * BLOCK, V7X_LANES), 1)
    ones = [jnp.where(lane < HEAD_DIM, 1.0, 0.0), jnp.where(lane < HEAD_DIM, 0.0, 1.0)]
    rows = PAIRS * BLOCK
    acc = None
    for parity, v in enumerate(_head_copies(v_group, kv_head)):
        v_wide = jnp.concatenate([v, ones[parity]], axis=1).astype(BF16)
        part = jnp.dot(e[parity * PAIRS:(parity + 1) * PAIRS].reshape(rows, 2 * BLOCK), v_wide,
                       preferred_element_type=F32)
        acc = part if acc is None else acc + part
    low_half = lax.broadcasted_iota(jnp.int32, (rows, V7X_LANES), 1) < HEAD_DIM
    denom = acc[:, V7X_LANES:] + jnp.where(low_half, sink_term[:PAIRS].reshape(rows, 1),
                                           sink_term[PAIRS:].reshape(rows, 1))
    attn = acc[:, :V7X_LANES] / denom
    attn = jnp.concatenate([attn[p * BLOCK:(p + 1) * BLOCK] for p in range(PAIRS)], axis=1)
    return (attn * (gate * _sigmoid(gate))).astype(BF16)


class _NextWeights:
    def __init__(self, layer, weights, sems, step, steps):
        self.layer, self.weights, self.sems, self.step, self.steps = layer, weights, sems, step, steps

    def _reads(self, slice_index):
        copies = []
        for n, (src, _, stage_in, _) in enumerate(self.weights):
            rows = stage_in.shape[0]
            copies.append(pltpu.make_async_copy(
                src.at[self.layer, pl.ds(slice_index * rows, rows), :], stage_in,
                self.sems.at[0, n]))
        return copies

    def _writes(self, slice_index):
        copies = []
        for n, (_, dst, _, stage_out) in enumerate(self.weights):
            rows = stage_out.shape[0]
            copies.append(pltpu.make_async_copy(
                stage_out, dst.at[pl.ds(slice_index * rows, rows), :], self.sems.at[1, n]))
        return copies

    def begin_step(self):
        step, last = self.step, self.steps - 1

        @pl.when(step == 0)
        def _():
            for _, _, _, stage_out in self.weights:
                stage_out[...] = jnp.zeros(stage_out.shape, stage_out.dtype)
            for copy in self._reads(0):
                copy.start()

        for copy in self._reads(step):
            copy.wait()

        @pl.when(step > 0)
        def _():
            for copy in self._writes(step - 1):
                copy.wait()

        for _, _, stage_in, stage_out in self.weights:
            stage_out[:, 0:stage_in.shape[1]] = stage_in[...].astype(BF16)
        for copy in self._writes(step):
            copy.start()
        for copy in self._reads(jnp.minimum(step + 1, last)):
            copy.start()

    def end_step(self):
        step, last = self.step, self.steps - 1

        @pl.when(step == last)
        def _():
            for copy in self._writes(last) + self._reads(last):
                copy.wait()


def _layer_kernel(convert_next, layer, steps, *refs):
    (x_ref, p_ref, win_ref, bin_all, wout_ref, wgate_ref, wple_ref, wpool_ref, pscale_all,
     bias_ref, sinks_ref, gain_all, lnb_all) = refs[:13]
    bin_ref, pscale_ref, gain_ref, lnb_ref = (r.at[layer:layer + 1]
                                              for r in (bin_all, pscale_all, gain_all, lnb_all))
    next_layer = layer + 1
    if convert_next:
        next_f32 = refs[13:16]
        o_ref = refs[16]
        next_bf16 = refs[17:20]
        kv_ref, u_ref, c_ref = refs[20:23]
        stage_in, stage_out, sems = refs[23:26], refs[26:29], refs[29]
    else:
        o_ref = refs[13]
        kv_ref, u_ref, c_ref = refs[14:17]
    j = pl.program_id(1)

    if convert_next:
        next_weights = _NextWeights(next_layer, list(zip(next_f32, next_bf16, stage_in, stage_out)),
                                    sems, pl.program_id(0) * pl.num_programs(1) + j, steps)
        next_weights.begin_step()

    @pl.when(j == 0)
    def _():
        kv_ref[0:BLOCK, :] = jnp.zeros((BLOCK, 2 * KV_COLS), F32)
        u_ref[0:POOL_TAIL, :] = jnp.zeros((POOL_TAIL, POOL_WIDTH), F32)

    xb = x_ref[...].astype(BF16)
    pb = p_ref[...].astype(BF16)

    def project(lo, hi):
        return (jnp.dot(xb, win_ref[:, lo:hi], preferred_element_type=F32) + bin_ref[:, lo:hi])

    q = (project(0, Q_END) * ATTN_SCALE).astype(BF16)
    kv_ref[BLOCK:, :] = project(Q_END, KV_END)
    ga = project(KV_END, GA_END)

    groups = []
    for blk in range(BLOCKS_PER_TILE):
        rows = slice(blk * BLOCK, (blk + 1) * BLOCK)
        first_key = jnp.where(j == 0, BLOCK, 0) if blk == 0 else 0
        kv = kv_ref[blk * BLOCK:(blk + 2) * BLOCK, :]
        for kv_head in range(N_KV_HEADS):
            cols = slice(kv_head * GROUP_COLS, (kv_head + 1) * GROUP_COLS)
            scores = _group_scores(q[rows, cols], kv[:, :V7X_LANES], kv_head)
            groups.append((rows, cols, kv_head, first_key, kv[:, V7X_LANES:], scores))
    ple_chunks = []
    gb_chunks = []
    for n, (rows, cols, kv_head, first_key, v_group, scores) in enumerate(groups):
        heads = slice(kv_head * KV_GROUP, (kv_head + 1) * KV_GROUP)
        pc = (n * POOL_SLICE) % POOL_WIDTH
        if n * POOL_SLICE < POOL_WIDTH:
            u_ref[POOL_TAIL:, pc:pc + POOL_SLICE] = project(GA_END + pc, GA_END + pc + POOL_SLICE)
        else:
            gb_chunks.append(project(U_END + pc, U_END + pc + POOL_SLICE))
        ec = slice(n * EMB_COLS, (n + 1) * EMB_COLS)
        gate_pre = jnp.dot(xb, wgate_ref[:, ec], preferred_element_type=F32)
        emb = jnp.dot(pb, wple_ref[:, ec], preferred_element_type=F32)
        c_ref[rows, cols] = _group_outputs(scores, ga[rows, cols], v_group, kv_head,
                                           first_key, sinks_ref[heads], bias_ref[heads])
        ple_chunks.append(_sigmoid(gate_pre) * emb)

    gb = jnp.concatenate(gb_chunks, axis=1)
    pos = j * TILE + lax.broadcasted_iota(jnp.int32, (TILE, 1), 0)
    for grp, window in enumerate(POOL_WINDOWS):
        c0 = grp * POOL_GROUP_DIM
        cols = slice(c0, c0 + POOL_GROUP_DIM)
        total = u_ref[:, cols]
        shift = 1
        while shift < window:
            total = total + pltpu.roll(total, shift, axis=0)
            shift *= 2
        total = total[POOL_TAIL:]
        u = u_ref[POOL_TAIL:, cols]
        count = jnp.minimum(pos + 1, window).astype(F32)
        diff = total * (1.0 / count) - u
        mixed = jnp.dot(diff.astype(BF16), wpool_ref[grp].astype(BF16),
                        preferred_element_type=F32)
        g = gb[:, cols]
        pooled = mixed * pscale_ref[:, cols] * (g * _sigmoid(g))
        c_ref[:, ATTN_WIDTH + c0:ATTN_WIDTH + c0 + POOL_GROUP_DIM] = pooled.astype(c_ref.dtype)

    kv_ref[0:BLOCK, :] = kv_ref[TILE:TILE + BLOCK, :]
    u_ref[0:POOL_TAIL, :] = u_ref[TILE:TILE + POOL_TAIL, :]

    ple = jnp.concatenate(ple_chunks, axis=1)
    for blk in range(BLOCKS_PER_TILE):
        rows = slice(blk * BLOCK, (blk + 1) * BLOCK)
        mix = jnp.dot(c_ref[rows, :], wout_ref[:, :D_MODEL], preferred_element_type=F32)
        y = DEEPNORM_ALPHA * x_ref[rows, :] + mix + ple[rows]
        o_ref[rows, :] = _layer_norm(y, gain_ref[...], lnb_ref[...])

    if convert_next:
        next_weights.end_step()


def _layer(x3, p4, sinks, band_bias, layer_weights, next_f32, b_in, w_ple, w_pool, pool_scale,
           ln_gain, ln_bias, layer):
    batch, seq, _ = x3.shape
    steps = batch * (seq // TILE)
    convert_next = next_f32 is not None
    tile = lambda b, j: (b, j, 0)
    whole = lambda b, j: (0, 0)
    per_layer = lambda b, j: (layer, 0, 0)
    per_layer4 = lambda b, j: (layer, 0, 0, 0)
    in_hbm = pl.BlockSpec(memory_space=pl.ANY)
    x_shape = jax.ShapeDtypeStruct((batch, seq, D_MODEL), F32)
    x_spec = pl.BlockSpec((None, TILE, D_MODEL), tile)
    scratch = [pltpu.VMEM((BLOCK + TILE, 2 * KV_COLS), F32),
               pltpu.VMEM((POOL_TAIL + TILE, POOL_WIDTH), F32),
               pltpu.VMEM((TILE, D_MODEL), BF16)]
    if convert_next:
        assert all(w.shape[1] % steps == 0 for w in next_f32)
        slices = [(w.shape[1] // steps, w.shape[2]) for w in next_f32]
        out_shape = [x_shape] + [jax.ShapeDtypeStruct(w.shape, BF16) for w in layer_weights]
        out_specs = [x_spec] + [in_hbm] * len(next_f32)
        scratch += [pltpu.VMEM(s, F32) for s in slices]
        scratch += [pltpu.VMEM((rows, w.shape[1]), BF16)
                    for (rows, _), w in zip(slices, layer_weights)]
        scratch += [pltpu.SemaphoreType.DMA((2, len(next_f32)))]
    else:
        out_shape, out_specs = x_shape, x_spec
    outs = pl.pallas_call(
        functools.partial(_layer_kernel, convert_next, layer, steps),
        grid=(batch, seq // TILE),
        in_specs=[
            x_spec,
            pl.BlockSpec((None, None, TILE, PLE_DIM), lambda b, j: (layer, b, j, 0)),
            _resident((D_MODEL, IN_COLS), whole),
            _resident((DEPTH, IN_COLS), whole),
            _resident((D_MODEL, PADDED_COLS), whole),
            _resident((D_MODEL, PADDED_COLS), whole),
            _resident((None, PLE_DIM, PADDED_COLS), per_layer),
            _resident((None, len(POOL_WINDOWS), POOL_GROUP_DIM, POOL_GROUP_DIM), per_layer4),
            _resident((DEPTH, POOL_WIDTH), whole),
            _resident((N_HEADS, BLOCK, 2 * BLOCK), lambda b, j: (0, 0, 0)),
            _resident((None, N_HEADS, 1, V7X_LANES), per_layer4),
            _resident((DEPTH, D_MODEL), whole),
            _resident((DEPTH, D_MODEL), whole),
        ] + ([in_hbm] * len(next_f32) if convert_next else []),
        out_specs=out_specs,
        out_shape=out_shape,
        scratch_shapes=scratch,
        compiler_params=pltpu.CompilerParams(
            dimension_semantics=("arbitrary", "arbitrary"),
            vmem_limit_bytes=V7X_VMEM_LIMIT_BYTES),
        name="hybrid_layer",
    )(x3, p4, layer_weights[0], b_in, layer_weights[1], layer_weights[2], w_ple, w_pool,
      pool_scale, band_bias, sinks, ln_gain, ln_bias, *(next_f32 if convert_next else ()))
    return (outs[0], tuple(outs[1:])) if convert_next else (outs, None)


def kernel(x, p, w_in, b_in, w_out, attn_sinks, rel_bias, w_pool, pool_scale, w_ple, w_gate_ple,
           ln_gain, ln_bias):
    batch, seq, _ = x.shape
    assert x.shape == (batch, seq, D_MODEL) and seq % TILE == 0
    assert w_in.shape == (DEPTH, D_MODEL, IN_COLS)

    layer_weights = (w_in[0].astype(BF16),
                     _to_padded_bf16(w_out, 1).reshape(D_MODEL, PADDED_COLS),
                     _to_padded_bf16(w_gate_ple, 1).reshape(D_MODEL, PADDED_COLS))
    next_f32 = (w_in, w_out, w_gate_ple)
    w_ple_b = _to_padded_bf16(w_ple, DEPTH)

    order = jnp.array(_group_order(), jnp.int32)
    band_bias = _band_bias(rel_bias)
    sinks = jnp.broadcast_to(attn_sinks[:, order, None, None], (DEPTH, N_HEADS, 1, V7X_LANES))
    for layer in range(DEPTH):
        x, layer_weights = _layer(x, p, sinks, band_bias, layer_weights,
                                  next_f32 if layer + 1 < DEPTH else None, b_in, w_ple_b,
                                  w_pool, pool_scale, ln_gain, ln_bias, layer)
    return x
```

```python
import functools
import math

import jax
import jax.numpy as jnp
import numpy as np
from jax import lax
from jax.experimental import pallas as pl
from jax.experimental.pallas import tpu as pltpu

D_MODEL = 2048
DEPTH = 4
PLE_DIM = 256
ATTN_WIDTH = D_MODEL // 2
POOL_WIDTH = D_MODEL - ATTN_WIDTH
HEAD_DIM = 64
N_HEADS = ATTN_WIDTH // HEAD_DIM
N_KV_HEADS = max(1, N_HEADS // 8)
KV_GROUP = N_HEADS // N_KV_HEADS
WINDOW = 128
BLOCK = WINDOW
POOL_WINDOWS = (2, 4, 8, 16)
POOL_GROUP_DIM = POOL_WIDTH // len(POOL_WINDOWS)
POOL_TAIL = 16
REL_BUCKETS = 32
REL_MAX_DIST = 128
LN_EPS = 1e-5
DEEPNORM_ALPHA = (2.0 * DEPTH) ** 0.25
Q_COLS = N_HEADS * HEAD_DIM
KV_COLS = N_KV_HEADS * HEAD_DIM
IN_COLS = Q_COLS + 2 * KV_COLS + ATTN_WIDTH + 2 * POOL_WIDTH
Q_END = Q_COLS
KV_END = Q_END + 2 * KV_COLS
GA_END = KV_END + ATTN_WIDTH
U_END = GA_END + POOL_WIDTH
ATTN_SCALE = 1.0 / math.sqrt(HEAD_DIM)
MASK_VALUE = -1e30

V7X_LANES = 128
PADDED_COLS = D_MODEL + V7X_LANES
V7X_VMEM_LIMIT_BYTES = 62 * 1024 * 1024

TILE = 2 * BLOCK
BLOCKS_PER_TILE = TILE // BLOCK
assert 2 * HEAD_DIM == V7X_LANES
PAIRS = KV_GROUP // 2
GROUP_COLS = KV_GROUP * HEAD_DIM
ATTN_GROUPS = BLOCKS_PER_TILE * N_KV_HEADS
EMB_COLS = D_MODEL // ATTN_GROUPS
POOL_SLICE = 2 * POOL_WIDTH // ATTN_GROUPS
assert POOL_WIDTH % POOL_SLICE == 0
PREP_ROWS = 1024
BF16 = jnp.bfloat16
F32 = jnp.float32


def _sigmoid(z):
    return 1.0 / (1.0 + jnp.exp(-z))


def _layer_norm(y, gain, bias):
    mu = jnp.mean(y, axis=-1, keepdims=True)
    yc = y - mu
    var = jnp.mean(yc * yc, axis=-1, keepdims=True)
    return yc * lax.rsqrt(var + LN_EPS) * gain + bias


def _resident(block_shape, index_map):
    return pl.BlockSpec(block_shape, index_map, pipeline_mode=pl.Buffered(1))


def _group_order():
    return [kv * KV_GROUP + 2 * pair + parity
            for kv in range(N_KV_HEADS) for parity in range(2) for pair in range(PAIRS)]


def _bias_kernel(rel_ref, bucket_ref, o_ref):
    bucket = bucket_ref[...]
    in_bucket = [bucket == b for b in range(REL_BUCKETS)]
    for slot, head in enumerate(_group_order()):
        row = jnp.zeros(bucket.shape, F32)
        for b in range(REL_BUCKETS):
            row = jnp.where(in_bucket[b], rel_ref[b, head], row)
        half = pltpu.roll(jnp.broadcast_to(row, (BLOCK, BLOCK)), 0, 1, stride=1, stride_axis=0)
        o_ref[slot] = jnp.concatenate([half, half], axis=1)


def _band_bias(rel_bias):
    dist = (BLOCK - np.arange(BLOCK)) % BLOCK
    max_exact = REL_BUCKETS // 2
    large = max_exact + np.floor(np.log(np.maximum(dist, 1) / max_exact)
                                 / math.log(REL_MAX_DIST / max_exact)
                                 * (REL_BUCKETS - max_exact)).astype(np.int32)
    large = np.minimum(large, REL_BUCKETS - 1)
    bucket = jnp.asarray(np.where(dist < max_exact, dist, large)[None, :], jnp.int32)
    return pl.pallas_call(
        _bias_kernel,
        in_specs=[pl.BlockSpec(memory_space=pltpu.SMEM),
                  pl.BlockSpec(memory_space=pltpu.VMEM)],
        out_specs=pl.BlockSpec(memory_space=pltpu.VMEM),
        out_shape=jax.ShapeDtypeStruct((N_HEADS, BLOCK, 2 * BLOCK), F32),
        name="band_bias",
    )(rel_bias, bucket)


def _to_bf16_kernel(w_ref, o_ref):
    o_ref[:, :D_MODEL] = w_ref[...].astype(BF16)
    o_ref[:, D_MODEL:] = jnp.zeros((o_ref.shape[0], PADDED_COLS - D_MODEL), BF16)


def _to_padded_bf16(w, depth):
    _, rows, cols = w.shape
    block_rows = min(rows, PREP_ROWS)
    assert cols == D_MODEL and rows % block_rows == 0
    return pl.pallas_call(
        _to_bf16_kernel,
        grid=(depth, rows // block_rows),
        in_specs=[pl.BlockSpec((None, block_rows, D_MODEL), lambda l, r: (l, r, 0))],
        out_specs=pl.BlockSpec((None, block_rows, PADDED_COLS), lambda l, r: (l, r, 0)),
        out_shape=jax.ShapeDtypeStruct((depth, rows, PADDED_COLS), BF16),
        name="to_padded_bf16",
    )(w)


def _head_copies(group, kv_head):
    low_half = lax.broadcasted_iota(jnp.int32, group.shape, 1) < HEAD_DIM
    if kv_head == 0:
        low = jnp.where(low_half, group, 0.0)
        high = pltpu.roll(low, HEAD_DIM, axis=1)
    else:
        high = jnp.where(low_half, 0.0, group)
        low = pltpu.roll(high, HEAD_DIM, axis=1)
    return low, high


def _group_scores(q_group, k_group, kv_head):
    q_rows = jnp.concatenate([q_group[:, p * V7X_LANES:(p + 1) * V7X_LANES]
                              for p in range(PAIRS)], axis=0)
    contract_lanes = (((1,), (1,)), ((), ()))
    scores = [lax.dot_general(q_rows, k.astype(BF16), contract_lanes, preferred_element_type=F32)
              .reshape(PAIRS, BLOCK, 2 * BLOCK) for k in _head_copies(k_group, kv_head)]
    return jnp.concatenate(scores, axis=0)


def _group_outputs(scores, gate, v_group, kv_head, first_key, sinks, bias):
    qq = lax.broadcasted_iota(jnp.int32, (BLOCK, 2 * BLOCK), 0)
    kk = lax.broadcasted_iota(jnp.int32, (BLOCK, 2 * BLOCK), 1)
    dist = qq + BLOCK - kk
    valid = (dist >= 0) & (dist < WINDOW) & (kk >= first_key)
    s = jnp.where(valid[None], scores + bias, MASK_VALUE)
    sink = sinks[:, :, :1]
    m = jnp.maximum(jnp.max(s, axis=-1, keepdims=True), sink)
    e = jnp.exp(s - m).astype(BF16)
    sink_term = jnp.exp(sink - m)

    lane = lax.broadcasted_iota(jnp.int32, (2 * BLOCK, V7X_LANES), 1)
    ones = [jnp.where(lane < HEAD_DIM, 1.0, 0.0), jnp.where(lane < HEAD_DIM, 0.0, 1.0)]
    rows = PAIRS * BLOCK
    acc = None
    for parity, v in enumerate(_head_copies(v_group, kv_head)):
        v_wide = jnp.concatenate([v, ones[parity]], axis=1).astype(BF16)
        part = jnp.dot(e[parity * PAIRS:(parity + 1) * PAIRS].reshape(rows, 2 * BLOCK), v_wide,
                       preferred_element_type=F32)
        acc = part if acc is None else acc + part
    low_half = lax.broadcasted_iota(jnp.int32, (rows, V7X_LANES), 1) < HEAD_DIM
    denom = acc[:, V7X_LANES:] + jnp.where(low_half, sink_term[:PAIRS].reshape(rows, 1),
                                           sink_term[PAIRS:].reshape(rows, 1))
    attn = acc[:, :V7X_LANES] / denom
    attn = jnp.concatenate([attn[p * BLOCK:(p + 1) * BLOCK] for p in range(PAIRS)], axis=1)
    return (attn * (gate * _sigmoid(gate))).astype(BF16)


class _NextWeights:
    def __init__(self, layer, weights, sems, step, steps):
        self.layer, self.weights, self.sems, self.step, self.steps = layer, weights, sems, step, steps

    def _reads(self, slice_index):
        copies = []
        for n, (src, _, stage_in, _) in enumerate(self.weights):
            rows = stage_in.shape[0]
            copies.append(pltpu.make_async_copy(
                src.at[self.layer, pl.ds(slice_index * rows, rows), :], stage_in,
                self.sems.at[0, n]))
        return copies

    def _writes(self, slice_index):
        copies = []
        for n, (_, dst, _, stage_out) in enumerate(self.weights):
            rows = stage_out.shape[0]
            copies.append(pltpu.make_async_copy(
                stage_out, dst.at[pl.ds(slice_index * rows, rows), :], self.sems.at[1, n]))
        return copies

    def begin_step(self):
        step, last = self.step, self.steps - 1

        @pl.when(step == 0)
        def _():
            for _, _, _, stage_out in self.weights:
                stage_out[...] = jnp.zeros(stage_out.shape, stage_out.dtype)
            for copy in self._reads(0):
                copy.start()

        for copy in self._reads(step):
            copy.wait()

        @pl.when(step > 0)
        def _():
            for copy in self._writes(step - 1):
                copy.wait()

        for _, _, stage_in, stage_out in self.weights:
            stage_out[:, 0:stage_in.shape[1]] = stage_in[...].astype(BF16)
        for copy in self._writes(step):
            copy.start()
        for copy in self._reads(jnp.minimum(step + 1, last)):
            copy.start()

    def end_step(self):
        step, last = self.step, self.steps - 1

        @pl.when(step == last)
        def _():
            for copy in self._writes(last) + self._reads(last):
                copy.wait()


def _layer_kernel(convert_next, layer, steps, *refs):
    (x_ref, p_ref, win_ref, bin_all, wout_ref, wgate_ref, wple_ref, wpool_ref, pscale_all,
     bias_ref, sinks_ref, gain_all, lnb_all) = refs[:13]
    bin_ref, pscale_ref, gain_ref, lnb_ref = (r.at[layer:layer + 1]
                                              for r in (bin_all, pscale_all, gain_all, lnb_all))
    next_layer = layer + 1
    if convert_next:
        next_f32 = refs[13:16]
        o_ref = refs[16]
        next_bf16 = refs[17:20]
        kv_ref, u_ref, c_ref = refs[20:23]
        stage_in, stage_out, sems = refs[23:26], refs[26:29], refs[29]
    else:
        o_ref = refs[13]
        kv_ref, u_ref, c_ref = refs[14:17]
    j = pl.program_id(1)

    if convert_next:
        next_weights = _NextWeights(next_layer, list(zip(next_f32, next_bf16, stage_in, stage_out)),
                                    sems, pl.program_id(0) * pl.num_programs(1) + j, steps)
        next_weights.begin_step()

    @pl.when(j == 0)
    def _():
        kv_ref[0:BLOCK, :] = jnp.zeros((BLOCK, 2 * KV_COLS), F32)
        u_ref[0:POOL_TAIL, :] = jnp.zeros((POOL_TAIL, POOL_WIDTH), F32)

    xb = x_ref[...].astype(BF16)
    pb = p_ref[...].astype(BF16)

    def project(lo, hi):
        return (jnp.dot(xb, win_ref[:, lo:hi], preferred_element_type=F32) + bin_ref[:, lo:hi])

    q = (project(0, Q_END) * ATTN_SCALE).astype(BF16)
    kv_ref[BLOCK:, :] = project(Q_END, KV_END)
    ga = project(KV_END, GA_END)

    groups = []
    for blk in range(BLOCKS_PER_TILE):
        rows = slice(blk * BLOCK, (blk + 1) * BLOCK)
        first_key = jnp.where(j == 0, BLOCK, 0) if blk == 0 else 0
        kv = kv_ref[blk * BLOCK:(blk + 2) * BLOCK, :]
        for kv_head in range(N_KV_HEADS):
            cols = slice(kv_head * GROUP_COLS, (kv_head + 1) * GROUP_COLS)
            scores = _group_scores(q[rows, cols], kv[:, :V7X_LANES], kv_head)
            groups.append((rows, cols, kv_head, first_key, kv[:, V7X_LANES:], scores))
    ple_chunks = []
    gb_chunks = []
    for n, (rows, cols, kv_head, first_key, v_group, scores) in enumerate(groups):
        heads = slice(kv_head * KV_GROUP, (kv_head + 1) * KV_GROUP)
        pc = (n * POOL_SLICE) % POOL_WIDTH
        if n * POOL_SLICE < POOL_WIDTH:
            u_ref[POOL_TAIL:, pc:pc + POOL_SLICE] = project(GA_END + pc, GA_END + pc + POOL_SLICE)
        else:
            gb_chunks.append(project(U_END + pc, U_END + pc + POOL_SLICE))
        ec = slice(n * EMB_COLS, (n + 1) * EMB_COLS)
        gate_pre = jnp.dot(xb, wgate_ref[:, ec], preferred_element_type=F32)
        emb = jnp.dot(pb, wple_ref[:, ec], preferred_element_type=F32)
        c_ref[rows, cols] = _group_outputs(scores, ga[rows, cols], v_group, kv_head,
                                           first_key, sinks_ref[heads], bias_ref[heads])
        ple_chunks.append(_sigmoid(gate_pre) * emb)

    gb = jnp.concatenate(gb_chunks, axis=1)
    pos = j * TILE + lax.broadcasted_iota(jnp.int32, (TILE, 1), 0)
    for grp, window in enumerate(POOL_WINDOWS):
        c0 = grp * POOL_GROUP_DIM
        cols = slice(c0, c0 + POOL_GROUP_DIM)
        total = u_ref[:, cols]
        shift = 1
        while shift < window:
            total = total + pltpu.roll(total, shift, axis=0)
            shift *= 2
        total = total[POOL_TAIL:]
        u = u_ref[POOL_TAIL:, cols]
        count = jnp.minimum(pos + 1, window).astype(F32)
        diff = total * (1.0 / count) - u
        mixed = jnp.dot(diff.astype(BF16), wpool_ref[grp].astype(BF16),
                        preferred_element_type=F32)
        g = gb[:, cols]
        pooled = mixed * pscale_ref[:, cols] * (g * _sigmoid(g))
        c_ref[:, ATTN_WIDTH + c0:ATTN_WIDTH + c0 + POOL_GROUP_DIM] = pooled.astype(c_ref.dtype)

    kv_ref[0:BLOCK, :] = kv_ref[TILE:TILE + BLOCK, :]
    u_ref[0:POOL_TAIL, :] = u_ref[TILE:TILE + POOL_TAIL, :]

    ple = jnp.concatenate(ple_chunks, axis=1)
    for blk in range(BLOCKS_PER_TILE):
        rows = slice(blk * BLOCK, (blk + 1) * BLOCK)
        mix = jnp.dot(c_ref[rows, :], wout_ref[:, :D_MODEL], preferred_element_type=F32)
        y = DEEPNORM_ALPHA * x_ref[rows, :] + mix + ple[rows]
        o_ref[rows, :] = _layer_norm(y, gain_ref[...], lnb_ref[...])

    if convert_next:
        next_weights.end_step()


def _layer(x3, p4, sinks, band_bias, layer_weights, next_f32, b_in, w_ple, w_pool, pool_scale,
           ln_gain, ln_bias, layer):
    batch, seq, _ = x3.shape
    steps = batch * (seq // TILE)
    convert_next = next_f32 is not None
    tile = lambda b, j: (b, j, 0)
    whole = lambda b, j: (0, 0)
    per_layer = lambda b, j: (layer, 0, 0)
    per_layer4 = lambda b, j: (layer, 0, 0, 0)
    in_hbm = pl.BlockSpec(memory_space=pl.ANY)
    x_shape = jax.ShapeDtypeStruct((batch, seq, D_MODEL), F32)
    x_spec = pl.BlockSpec((None, TILE, D_MODEL), tile)
    scratch = [pltpu.VMEM((BLOCK + TILE, 2 * KV_COLS), F32),
               pltpu.VMEM((POOL_TAIL + TILE, POOL_WIDTH), F32),
               pltpu.VMEM((TILE, D_MODEL), BF16)]
    if convert_next:
        assert all(w.shape[1] % steps == 0 for w in next_f32)
        slices = [(w.shape[1] // steps, w.shape[2]) for w in next_f32]
        out_shape = [x_shape] + [jax.ShapeDtypeStruct(w.shape, BF16) for w in layer_weights]
        out_specs = [x_spec] + [in_hbm] * len(next_f32)
        scratch += [pltpu.VMEM(s, F32) for s in slices]
        scratch += [pltpu.VMEM((rows, w.shape[1]), BF16)
                    for (rows, _), w in zip(slices, layer_weights)]
        scratch += [pltpu.SemaphoreType.DMA((2, len(next_f32)))]
    else:
        out_shape, out_specs = x_shape, x_spec
    outs = pl.pallas_call(
        functools.partial(_layer_kernel, convert_next, layer, steps),
        grid=(batch, seq // TILE),
        in_specs=[
            x_spec,
            pl.BlockSpec((None, None, TILE, PLE_DIM), lambda b, j: (layer, b, j, 0)),
            _resident((D_MODEL, IN_COLS), whole),
            _resident((DEPTH, IN_COLS), whole),
            _resident((D_MODEL, PADDED_COLS), whole),
            _resident((D_MODEL, PADDED_COLS), whole),
            _resident((None, PLE_DIM, PADDED_COLS), per_layer),
            _resident((None, len(POOL_WINDOWS), POOL_GROUP_DIM, POOL_GROUP_DIM), per_layer4),
            _resident((DEPTH, POOL_WIDTH), whole),
            _resident((N_HEADS, BLOCK, 2 * BLOCK), lambda b, j: (0, 0, 0)),
            _resident((None, N_HEADS, 1, V7X_LANES), per_layer4),
            _resident((DEPTH, D_MODEL), whole),
            _resident((DEPTH, D_MODEL), whole),
        ] + ([in_hbm] * len(next_f32) if convert_next else []),
        out_specs=out_specs,
        out_shape=out_shape,
        scratch_shapes=scratch,
        compiler_params=pltpu.CompilerParams(
            dimension_semantics=("arbitrary", "arbitrary"),
            vmem_limit_bytes=V7X_VMEM_LIMIT_BYTES),
        name="hybrid_layer",
    )(x3, p4, layer_weights[0], b_in, layer_weights[1], layer_weights[2], w_ple, w_pool,
      pool_scale, band_bias, sinks, ln_gain, ln_bias, *(next_f32 if convert_next else ()))
    return (outs[0], tuple(outs[1:])) if convert_next else (outs, None)


def kernel(x, p, w_in, b_in, w_out, attn_sinks, rel_bias, w_pool, pool_scale, w_ple, w_gate_ple,
           ln_gain, ln_bias):
    batch, seq, _ = x.shape
    assert x.shape == (batch, seq, D_MODEL) and seq % TILE == 0
    assert w_in.shape == (DEPTH, D_MODEL, IN_COLS)

    layer_weights = (w_in[0].astype(BF16),
                     _to_padded_bf16(w_out, 1).reshape(D_MODEL, PADDED_COLS),
                     _to_padded_bf16(w_gate_ple, 1).reshape(D_MODEL, PADDED_COLS))
    next_f32 = (w_in, w_out, w_gate_ple)
    w_ple_b = _to_padded_bf16(w_ple, DEPTH)

    order = jnp.array(_group_order(), jnp.int32)
    band_bias = _band_bias(rel_bias)
    sinks = jnp.broadcast_to(attn_sinks[:, order, None, None], (DEPTH, N_HEADS, 1, V7X_LANES))
    for layer in range(DEPTH):
        x, layer_weights = _layer(x, p, sinks, band_bias, layer_weights,
                                  next_f32 if layer + 1 < DEPTH else None, b_in, w_ple_b,
                                  w_pool, pool_scale, ln_gain, ln_bias, layer)
    return x
```

```python
import functools
import math

import jax
import jax.numpy as jnp
import numpy as np
from jax import lax
from jax.experimental import pallas as pl
from jax.experimental.pallas import tpu as pltpu

D_MODEL = 2048
DEPTH = 4
PLE_DIM = 256
ATTN_WIDTH = D_MODEL // 2
POOL_WIDTH = D_MODEL - ATTN_WIDTH
HEAD_DIM = 64
N_HEADS = ATTN_WIDTH // HEAD_DIM
N_KV_HEADS = max(1, N_HEADS // 8)
KV_GROUP = N_HEADS // N_KV_HEADS
WINDOW = 128
BLOCK = WINDOW
POOL_WINDOWS = (2, 4, 8, 16)
POOL_GROUP_DIM = POOL_WIDTH // len(POOL_WINDOWS)
POOL_TAIL = 16
REL_BUCKETS = 32
REL_MAX_DIST = 128
LN_EPS = 1e-5
DEEPNORM_ALPHA = (2.0 * DEPTH) ** 0.25
Q_COLS = N_HEADS * HEAD_DIM
KV_COLS = N_KV_HEADS * HEAD_DIM
IN_COLS = Q_COLS + 2 * KV_COLS + ATTN_WIDTH + 2 * POOL_WIDTH
Q_END = Q_COLS
KV_END = Q_END + 2 * KV_COLS
GA_END = KV_END + ATTN_WIDTH
U_END = GA_END + POOL_WIDTH
ATTN_SCALE = 1.0 / math.sqrt(HEAD_DIM)
MASK_VALUE = -1e30

V7X_LANES = 128
PADDED_COLS = D_MODEL + V7X_LANES
V7X_VMEM_LIMIT_BYTES = 62 * 1024 * 1024

TILE = 2 * BLOCK
BLOCKS_PER_TILE = TILE // BLOCK
assert 2 * HEAD_DIM == V7X_LANES
PAIRS = KV_GROUP // 2
GROUP_COLS = KV_GROUP * HEAD_DIM
ATTN_GROUPS = BLOCKS_PER_TILE * N_KV_HEADS
EMB_COLS = D_MODEL // ATTN_GROUPS
POOL_SLICE = 2 * POOL_WIDTH // ATTN_GROUPS
assert POOL_WIDTH % POOL_SLICE == 0
PREP_ROWS = 1024
BF16 = jnp.bfloat16
F32 = jnp.float32


def _sigmoid(z):
    return 1.0 / (1.0 + jnp.exp(-z))


def _layer_norm(y, gain, bias):
    mu = jnp.mean(y, axis=-1, keepdims=True)
    yc = y - mu
    var = jnp.mean(yc * yc, axis=-1, keepdims=True)
    return yc * lax.rsqrt(var + LN_EPS) * gain + bias


def _resident(block_shape, index_map):
    return pl.BlockSpec(block_shape, index_map, pipeline_mode=pl.Buffered(1))


def _group_order():
    return [kv * KV_GROUP + 2 * pair + parity
            for kv in range(N_KV_HEADS) for parity in range(2) for pair in range(PAIRS)]


def _bias_kernel(rel_ref, bucket_ref, o_ref):
    bucket = bucket_ref[...]
    in_bucket = [bucket == b for b in range(REL_BUCKETS)]
    for slot, head in enumerate(_group_order()):
        row = jnp.zeros(bucket.shape, F32)
        for b in range(REL_BUCKETS):
            row = jnp.where(in_bucket[b], rel_ref[b, head], row)
        half = pltpu.roll(jnp.broadcast_to(row, (BLOCK, BLOCK)), 0, 1, stride=1, stride_axis=0)
        o_ref[slot] = jnp.concatenate([half, half], axis=1)


def _band_bias(rel_bias):
    dist = (BLOCK - np.arange(BLOCK)) % BLOCK
    max_exact = REL_BUCKETS // 2
    large = max_exact + np.floor(np.log(np.maximum(dist, 1) / max_exact)
                                 / math.log(REL_MAX_DIST / max_exact)
                                 * (REL_BUCKETS - max_exact)).astype(np.int32)
    large = np.minimum(large, REL_BUCKETS - 1)
    bucket = jnp.asarray(np.where(dist < max_exact, dist, large)[None, :], jnp.int32)
    return pl.pallas_call(
        _bias_kernel,
        in_specs=[pl.BlockSpec(memory_space=pltpu.SMEM),
                  pl.BlockSpec(memory_space=pltpu.VMEM)],
        out_specs=pl.BlockSpec(memory_space=pltpu.VMEM),
        out_shape=jax.ShapeDtypeStruct((N_HEADS, BLOCK, 2 * BLOCK), F32),
        name="band_bias",
    )(rel_bias, bucket)


def _to_bf16_kernel(*refs):
    count = len(refs) // 2
    for w_ref, o_ref in zip(refs[:count], refs[count:]):
        o_ref[:, :D_MODEL] = w_ref[...].astype(BF16)
        o_ref[:, D_MODEL:] = jnp.zeros((o_ref.shape[0], PADDED_COLS - D_MODEL), BF16)


def _to_padded_bf16(weights, depth):
    _, rows, cols = weights[0].shape
    assert all(w.shape == weights[0].shape for w in weights)
    block_rows = min(rows, PREP_ROWS // len(weights))
    assert cols == D_MODEL and rows % block_rows == 0
    in_spec = pl.BlockSpec((None, block_rows, D_MODEL), lambda l, r: (l, r, 0))
    out_spec = pl.BlockSpec((None, block_rows, PADDED_COLS), lambda l, r: (l, r, 0))
    return pl.pallas_call(
        _to_bf16_kernel,
        grid=(depth, rows // block_rows),
        in_specs=[in_spec] * len(weights),
        out_specs=[out_spec] * len(weights),
        out_shape=[jax.ShapeDtypeStruct((depth, rows, PADDED_COLS), BF16)] * len(weights),
        name="to_padded_bf16",
    )(*weights)


def _head_copies(group, kv_head):
    low_half = lax.broadcasted_iota(jnp.int32, group.shape, 1) < HEAD_DIM
    if kv_head == 0:
        low = jnp.where(low_half, group, 0.0)
        high = pltpu.roll(low, HEAD_DIM, axis=1)
    else:
        high = jnp.where(low_half, 0.0, group)
        low = pltpu.roll(high, HEAD_DIM, axis=1)
    return low, high


def _group_scores(q_group, k_group, kv_head):
    q_rows = jnp.concatenate([q_group[:, p * V7X_LANES:(p + 1) * V7X_LANES]
                              for p in range(PAIRS)], axis=0)
    contract_lanes = (((1,), (1,)), ((), ()))
    scores = [lax.dot_general(q_rows, k.astype(BF16), contract_lanes, preferred_element_type=F32)
              .reshape(PAIRS, BLOCK, 2 * BLOCK) for k in _head_copies(k_group, kv_head)]
    return jnp.concatenate(scores, axis=0)


def _group_outputs(scores, gate, v_group, kv_head, first_key, sinks, bias):
    qq = lax.broadcasted_iota(jnp.int32, (BLOCK, 2 * BLOCK), 0)
    kk = lax.broadcasted_iota(jnp.int32, (BLOCK, 2 * BLOCK), 1)
    dist = qq + BLOCK - kk
    valid = (dist >= 0) & (dist < WINDOW) & (kk >= first_key)
    s = jnp.where(valid[None], scores + bias, MASK_VALUE)
    sink = sinks[:, :, :1]
    m = jnp.maximum(jnp.max(s, axis=-1, keepdims=True), sink)
    e = jnp.exp(s - m).astype(BF16)
    sink_term = jnp.exp(sink - m)

    lane = lax.broadcasted_iota(jnp.int32, (2 * BLOCK, V7X_LANES), 1)
    ones = [jnp.where(lane < HEAD_DIM, 1.0, 0.0), jnp.where(lane < HEAD_DIM, 0.0, 1.0)]
    rows = PAIRS * BLOCK
    acc = None
    for parity, v in enumerate(_head_copies(v_group, kv_head)):
        v_wide = jnp.concatenate([v, ones[parity]], axis=1).astype(BF16)
        part = jnp.dot(e[parity * PAIRS:(parity + 1) * PAIRS].reshape(rows, 2 * BLOCK), v_wide,
                       preferred_element_type=F32)
        acc = part if acc is None else acc + part
    low_half = lax.broadcasted_iota(jnp.int32, (rows, V7X_LANES), 1) < HEAD_DIM
    denom = acc[:, V7X_LANES:] + jnp.where(low_half, sink_term[:PAIRS].reshape(rows, 1),
                                           sink_term[PAIRS:].reshape(rows, 1))
    attn = acc[:, :V7X_LANES] / denom
    attn = jnp.concatenate([attn[p * BLOCK:(p + 1) * BLOCK] for p in range(PAIRS)], axis=1)
    return (attn * (gate * _sigmoid(gate))).astype(BF16)


class _NextWeights:
    def __init__(self, layer, weights, sems, step, steps):
        self.layer, self.weights, self.sems, self.step, self.steps = layer, weights, sems, step, steps

    def _reads(self, slice_index):
        copies = []
        for n, (src, _, stage_in, _) in enumerate(self.weights):
            rows = stage_in.shape[0]
            copies.append(pltpu.make_async_copy(
                src.at[self.layer, pl.ds(slice_index * rows, rows), :], stage_in,
                self.sems.at[0, n]))
        return copies

    def _writes(self, slice_index):
        copies = []
        for n, (_, dst, _, stage_out) in enumerate(self.weights):
            rows = stage_out.shape[0]
            copies.append(pltpu.make_async_copy(
                stage_out, dst.at[pl.ds(slice_index * rows, rows), :], self.sems.at[1, n]))
        return copies

    def begin_step(self):
        step, last = self.step, self.steps - 1

        @pl.when(step == 0)
        def _():
            for _, _, _, stage_out in self.weights:
                stage_out[...] = jnp.zeros(stage_out.shape, stage_out.dtype)
            for copy in self._reads(0):
                copy.start()

        for copy in self._reads(step):
            copy.wait()

        @pl.when(step > 0)
        def _():
            for copy in self._writes(step - 1):
                copy.wait()

        for _, _, stage_in, stage_out in self.weights:
            stage_out[:, 0:stage_in.shape[1]] = stage_in[...].astype(BF16)
        for copy in self._writes(step):
            copy.start()
        for copy in self._reads(jnp.minimum(step + 1, last)):
            copy.start()

    def end_step(self):
        step, last = self.step, self.steps - 1

        @pl.when(step == last)
        def _():
            for copy in self._writes(last) + self._reads(last):
                copy.wait()


def _layer_kernel(convert_next, layer, steps, *refs):
    (x_ref, p_ref, win_ref, bin_all, wout_ref, wgate_ref, wple_ref, wpool_ref, pscale_all,
     bias_ref, sinks_ref, gain_all, lnb_all) = refs[:13]
    bin_ref, pscale_ref, gain_ref, lnb_ref = (r.at[layer:layer + 1]
                                              for r in (bin_all, pscale_all, gain_all, lnb_all))
    next_layer = layer + 1
    if convert_next:
        next_f32 = refs[13:16]
        o_ref = refs[16]
        next_bf16 = refs[17:20]
        kv_ref, u_ref, c_ref = refs[20:23]
        stage_in, stage_out, sems = refs[23:26], refs[26:29], refs[29]
    else:
        o_ref = refs[13]
        kv_ref, u_ref, c_ref = refs[14:17]
    j = pl.program_id(1)

    if convert_next:
        next_weights = _NextWeights(next_layer, list(zip(next_f32, next_bf16, stage_in, stage_out)),
                                    sems, pl.program_id(0) * pl.num_programs(1) + j, steps)
        next_weights.begin_step()

    @pl.when(j == 0)
    def _():
        kv_ref[0:BLOCK, :] = jnp.zeros((BLOCK, 2 * KV_COLS), F32)
        u_ref[0:POOL_TAIL, :] = jnp.zeros((POOL_TAIL, POOL_WIDTH), F32)

    xb = x_ref[...].astype(BF16)
    pb = p_ref[...].astype(BF16)

    def project(lo, hi):
        return (jnp.dot(xb, win_ref[:, lo:hi], preferred_element_type=F32) + bin_ref[:, lo:hi])

    q = (project(0, Q_END) * ATTN_SCALE).astype(BF16)
    kv_ref[BLOCK:, :] = project(Q_END, KV_END)
    ga = project(KV_END, GA_END)

    groups = []
    for blk in range(BLOCKS_PER_TILE):
        rows = slice(blk * BLOCK, (blk + 1) * BLOCK)
        first_key = jnp.where(j == 0, BLOCK, 0) if blk == 0 else 0
        kv = kv_ref[blk * BLOCK:(blk + 2) * BLOCK, :]
        for kv_head in range(N_KV_HEADS):
            cols = slice(kv_head * GROUP_COLS, (kv_head + 1) * GROUP_COLS)
            scores = _group_scores(q[rows, cols], kv[:, :V7X_LANES], kv_head)
            groups.append((rows, cols, kv_head, first_key, kv[:, V7X_LANES:], scores))
    ple_chunks = []
    gb_chunks = []
    for n, (rows, cols, kv_head, first_key, v_group, scores) in enumerate(groups):
        heads = slice(kv_head * KV_GROUP, (kv_head + 1) * KV_GROUP)
        pc = (n * POOL_SLICE) % POOL_WIDTH
        if n * POOL_SLICE < POOL_WIDTH:
            u_ref[POOL_TAIL:, pc:pc + POOL_SLICE] = project(GA_END + pc, GA_END + pc + POOL_SLICE)
        else:
            gb_chunks.append(project(U_END + pc, U_END + pc + POOL_SLICE))
        ec = slice(n * EMB_COLS, (n + 1) * EMB_COLS)
        gate_pre = jnp.dot(xb, wgate_ref[:, ec], preferred_element_type=F32)
        emb = jnp.dot(pb, wple_ref[:, ec], preferred_element_type=F32)
        c_ref[rows, cols] = _group_outputs(scores, ga[rows, cols], v_group, kv_head,
                                           first_key, sinks_ref[heads], bias_ref[heads])
        ple_chunks.append(_sigmoid(gate_pre) * emb)

    gb = jnp.concatenate(gb_chunks, axis=1)
    pos = j * TILE + lax.broadcasted_iota(jnp.int32, (TILE, 1), 0)
    for grp, window in enumerate(POOL_WINDOWS):
        c0 = grp * POOL_GROUP_DIM
        cols = slice(c0, c0 + POOL_GROUP_DIM)
        total = u_ref[:, cols]
        shift = 1
        while shift < window:
            total = total + pltpu.roll(total, shift, axis=0)
            shift *= 2
        total = total[POOL_TAIL:]
        u = u_ref[POOL_TAIL:, cols]
        count = jnp.minimum(pos + 1, window).astype(F32)
        diff = total * (1.0 / count) - u
        mixed = jnp.dot(diff.astype(BF16), wpool_ref[grp].astype(BF16),
                        preferred_element_type=F32)
        g = gb[:, cols]
        pooled = mixed * pscale_ref[:, cols] * (g * _sigmoid(g))
        c_ref[:, ATTN_WIDTH + c0:ATTN_WIDTH + c0 + POOL_GROUP_DIM] = pooled.astype(c_ref.dtype)

    kv_ref[0:BLOCK, :] = kv_ref[TILE:TILE + BLOCK, :]
    u_ref[0:POOL_TAIL, :] = u_ref[TILE:TILE + POOL_TAIL, :]

    ple = jnp.concatenate(ple_chunks, axis=1)
    for blk in range(BLOCKS_PER_TILE):
        rows = slice(blk * BLOCK, (blk + 1) * BLOCK)
        mix = jnp.dot(c_ref[rows, :], wout_ref[:, :D_MODEL], preferred_element_type=F32)
        y = DEEPNORM_ALPHA * x_ref[rows, :] + mix + ple[rows]
        o_ref[rows, :] = _layer_norm(y, gain_ref[...], lnb_ref[...])

    if convert_next:
        next_weights.end_step()


def _layer(x3, p4, sinks, band_bias, layer_weights, next_f32, b_in, w_ple, w_pool, pool_scale,
           ln_gain, ln_bias, layer):
    batch, seq, _ = x3.shape
    steps = batch * (seq // TILE)
    convert_next = next_f32 is not None
    tile = lambda b, j: (b, j, 0)
    whole = lambda b, j: (0, 0)
    per_layer = lambda b, j: (layer, 0, 0)
    per_layer4 = lambda b, j: (layer, 0, 0, 0)
    in_hbm = pl.BlockSpec(memory_space=pl.ANY)
    x_shape = jax.ShapeDtypeStruct((batch, seq, D_MODEL), F32)
    x_spec = pl.BlockSpec((None, TILE, D_MODEL), tile)
    scratch = [pltpu.VMEM((BLOCK + TILE, 2 * KV_COLS), F32),
               pltpu.VMEM((POOL_TAIL + TILE, POOL_WIDTH), F32),
               pltpu.VMEM((TILE, D_MODEL), BF16)]
    if convert_next:
        assert all(w.shape[1] % steps == 0 for w in next_f32)
        slices = [(w.shape[1] // steps, w.shape[2]) for w in next_f32]
        out_shape = [x_shape] + [jax.ShapeDtypeStruct(w.shape, BF16) for w in layer_weights]
        out_specs = [x_spec] + [in_hbm] * len(next_f32)
        scratch += [pltpu.VMEM(s, F32) for s in slices]
        scratch += [pltpu.VMEM((rows, w.shape[1]), BF16)
                    for (rows, _), w in zip(slices, layer_weights)]
        scratch += [pltpu.SemaphoreType.DMA((2, len(next_f32)))]
    else:
        out_shape, out_specs = x_shape, x_spec
    outs = pl.pallas_call(
        functools.partial(_layer_kernel, convert_next, layer, steps),
        grid=(batch, seq // TILE),
        in_specs=[
            x_spec,
            pl.BlockSpec((None, None, TILE, PLE_DIM), lambda b, j: (layer, b, j, 0)),
            _resident((D_MODEL, IN_COLS), whole),
            _resident((DEPTH, IN_COLS), whole),
            _resident((D_MODEL, PADDED_COLS), whole),
            _resident((D_MODEL, PADDED_COLS), whole),
            _resident((None, PLE_DIM, PADDED_COLS), per_layer),
            _resident((None, len(POOL_WINDOWS), POOL_GROUP_DIM, POOL_GROUP_DIM), per_layer4),
            _resident((DEPTH, POOL_WIDTH), whole),
            _resident((N_HEADS, BLOCK, 2 * BLOCK), lambda b, j: (0, 0, 0)),
            _resident((None, N_HEADS, 1, V7X_LANES), per_layer4),
            _resident((DEPTH, D_MODEL), whole),
            _resident((DEPTH, D_MODEL), whole),
        ] + ([in_hbm] * len(next_f32) if convert_next else []),
        out_specs=out_specs,
        out_shape=out_shape,
        scratch_shapes=scratch,
        compiler_params=pltpu.CompilerParams(
            dimension_semantics=("arbitrary", "arbitrary"),
            vmem_limit_bytes=V7X_VMEM_LIMIT_BYTES),
        name="hybrid_layer",
    )(x3, p4, layer_weights[0], b_in, layer_weights[1], layer_weights[2], w_ple, w_pool,
      pool_scale, band_bias, sinks, ln_gain, ln_bias, *(next_f32 if convert_next else ()))
    return (outs[0], tuple(outs[1:])) if convert_next else (outs, None)


def kernel(x, p, w_in, b_in, w_out, attn_sinks, rel_bias, w_pool, pool_scale, w_ple, w_gate_ple,
           ln_gain, ln_bias):
    batch, seq, _ = x.shape
    assert x.shape == (batch, seq, D_MODEL) and seq % TILE == 0
    assert w_in.shape == (DEPTH, D_MODEL, IN_COLS)

    w_out0, w_gate0 = _to_padded_bf16([w_out, w_gate_ple], 1)
    layer_weights = (w_in[0].astype(BF16), w_out0.reshape(D_MODEL, PADDED_COLS),
                     w_gate0.reshape(D_MODEL, PADDED_COLS))
    next_f32 = (w_in, w_out, w_gate_ple)
    w_ple_b, = _to_padded_bf16([w_ple], DEPTH)

    order = jnp.array(_group_order(), jnp.int32)
    band_bias = _band_bias(rel_bias)
    sinks = jnp.broadcast_to(attn_sinks[:, order, None, None], (DEPTH, N_HEADS, 1, V7X_LANES))
    for layer in range(DEPTH):
        x, layer_weights = _layer(x, p, sinks, band_bias, layer_weights,
                                  next_f32 if layer + 1 < DEPTH else None, b_in, w_ple_b,
                                  w_pool, pool_scale, ln_gain, ln_bias, layer)
    return x
```
